```python
import jax, jax.numpy as jnp
from jax import lax
import numpy as np

D_MODEL = 2048
BATCH = 1
SEQ = 8192
DEPTH = 1
DEC_BATCH = 16
DEC_SEQ = 32
PAST_LEN = 2048

CHUNK = 64
D_MIX = D_MODEL
D_LRU = D_MIX // 2
D_SC = D_MIX - D_LRU
LRU_HEADS = 16
LRU_HEAD_DIM = D_LRU // LRU_HEADS
LRU_CONV = 4
LRU_C = 8.0
SC_CONV = 3
N_KEYS = 128
N_EXPERTS = N_KEYS * N_KEYS
PEER_HEADS = 8
PEER_TOPK = 16
D_KEY = 256
D_HALF = D_KEY // 2
PEER_BLOCK = 128
D_IN = 2 * D_LRU + 3 * D_SC
EPS = 1e-6

kernel_name = 'hymba_rglru_shortconv_peer_stream_step'


def rmsnorm(x, g):
    xf = x.astype(jnp.float32)
    y = xf * lax.rsqrt(jnp.mean(xf * xf, axis=-1, keepdims=True) + EPS)
    return (y * g.astype(jnp.float32)).astype(x.dtype)


def causal_dwconv(x, buf, w, b=None):
    width = w.shape[0]
    T = x.shape[1]
    xp = jnp.concatenate([buf.astype(x.dtype), x], axis=1)
    y = xp[:, 0:T] * w[0]
    for k in range(1, width):
        y = y + xp[:, k:k + T] * w[k]
    if b is not None:
        y = y + b
    return y, xp[:, xp.shape[1] - (width - 1):]


def rg_lru(x, h0, w_a, b_a, w_x, b_x, lam, reset_first):
    Bsz, T, _ = x.shape
    xh = x.reshape(Bsz, T, LRU_HEADS, LRU_HEAD_DIM)
    r = jax.nn.sigmoid(jnp.einsum('bthi,hij->bthj', xh, w_a.astype(jnp.float32)).reshape(Bsz, T, D_LRU) + b_a.astype(jnp.float32))
    i = jax.nn.sigmoid(jnp.einsum('bthi,hij->bthj', xh, w_x.astype(jnp.float32)).reshape(Bsz, T, D_LRU) + b_x.astype(jnp.float32))
    log_a = -LRU_C * r * jax.nn.softplus(-lam.astype(jnp.float32))
    a = jnp.exp(log_a)
    mult = jnp.sqrt(-jnp.expm1(2.0 * log_a))
    if reset_first:
        mult = mult.at[:, 0].set(1.0)
    bterm = mult * (i * x)

    def combine(lhs, rhs):
        a1, b1 = lhs
        a2, b2 = rhs
        return a1 * a2, a2 * b1 + b2

    a_cum, h = lax.associative_scan(combine, (a, bterm), axis=1)
    h = h + a_cum * h0[:, None, :]
    return h, h[:, -1]


def token_mixers(xn, h0, c4, c3, w_in, conv_lru_w, conv_lru_b, lru_w_a, lru_b_a, lru_w_x, lru_b_x,
                 lru_lambda, conv_sc_w, g_out_lru, g_out_sc, w_out, reset_first):
    z = xn @ w_in
    xr, yr, bg, cg, xs = jnp.split(z, [D_LRU, 2 * D_LRU, 2 * D_LRU + D_SC, 2 * D_LRU + 2 * D_SC], axis=-1)
    xr, c4_new = causal_dwconv(xr, c4, conv_lru_w, conv_lru_b)
    h, h_new = rg_lru(xr.astype(jnp.float32), h0.astype(jnp.float32), lru_w_a, lru_b_a, lru_w_x, lru_b_x,
                      lru_lambda, reset_first)
    o_lru = h.astype(xn.dtype) * jax.nn.gelu(yr)
    uc, c3_new = causal_dwconv(cg * xs, c3, conv_sc_w)
    o_sc = bg * uc
    o = jnp.concatenate([rmsnorm(o_lru, g_out_lru), rmsnorm(o_sc, g_out_sc)], axis=-1)
    return o @ w_out, h_new.astype(h0.dtype), c4_new, c3_new


def peer(xn, w_q, sub_k1, sub_k2, u_tab, v_tab):
    shp = xn.shape
    xt = xn.reshape(-1, D_MODEL)
    n = xt.shape[0]
    nb = -(-n // PEER_BLOCK)
    xt = jnp.pad(xt, ((0, nb * PEER_BLOCK - n), (0, 0))).reshape(nb, PEER_BLOCK, D_MODEL)
    k1 = sub_k1.astype(jnp.float32)
    k2 = sub_k2.astype(jnp.float32)

    def block(xb):
        q = (xb @ w_q).astype(jnp.float32).reshape(PEER_BLOCK, PEER_HEADS, 2, D_HALF)
        s1 = jnp.einsum('thd,kd->thk', q[:, :, 0], k1)
        s2 = jnp.einsum('thd,kd->thk', q[:, :, 1], k2)
        v1, i1 = lax.top_k(s1, PEER_TOPK)
        v2, i2 = lax.top_k(s2, PEER_TOPK)
        cand_s = (v1[..., :, None] + v2[..., None, :]).reshape(PEER_BLOCK, PEER_HEADS, PEER_TOPK * PEER_TOPK)
        cand_i = (i1[..., :, None] * N_KEYS + i2[..., None, :]).reshape(PEER_BLOCK, PEER_HEADS, PEER_TOPK * PEER_TOPK)
        top_s, pos = lax.top_k(cand_s, PEER_TOPK)
        idx = jnp.take_along_axis(cand_i, pos, axis=-1)
        g = jax.nn.softmax(top_s, axis=-1)
        act = jax.nn.gelu(jnp.einsum('thkd,td->thk', u_tab[idx], xb).astype(jnp.float32))
        wgt = (g * act).astype(xb.dtype)
        return jnp.einsum('thk,thkd->td', wgt, v_tab[idx])

    out = lax.map(block, xt).reshape(nb * PEER_BLOCK, D_MODEL)[:n]
    return out.reshape(shp)


def setup_inputs(seed: int = 0) -> dict:
    key = jax.random.key(seed)
    ks = jax.random.split(key, 32)
    nrm = lambda k, s, sc: jax.random.normal(k, s, jnp.float32) * sc
    gain = lambda k, s: 1.0 + 0.01 * jax.random.normal(k, s, jnp.float32)
    ac = jax.random.uniform(ks[10], (DEPTH, D_LRU), jnp.float32, minval=0.9, maxval=0.999)
    a0 = ac ** (1.0 / LRU_C)
    lam = jnp.log(a0) - jnp.log1p(-a0)
    return {
        'x_prompt': nrm(ks[0], (BATCH, SEQ, D_MODEL), 1.0),
        'x_sample': nrm(ks[1], (DEC_BATCH, DEC_SEQ, D_MODEL), 1.0),
        'state_lru': nrm(ks[2], (DEPTH, DEC_BATCH, D_LRU), 0.5),
        'cache_conv_lru': nrm(ks[3], (DEPTH, DEC_BATCH, LRU_CONV - 1, D_LRU), 1.0),
        'cache_conv_short': nrm(ks[4], (DEPTH, DEC_BATCH, SC_CONV - 1, D_SC), 1.0),
        'g_mix': gain(ks[5], (DEPTH, D_MODEL)),
        'w_in': nrm(ks[6], (DEPTH, D_MODEL, D_IN), D_MODEL ** -0.5),
        'conv_lru_w': nrm(ks[7], (DEPTH, LRU_CONV, D_LRU), LRU_CONV ** -0.5),
        'conv_lru_b': nrm(ks[8], (DEPTH, D_LRU), 0.01),
        'lru_w_a': nrm(ks[9], (DEPTH, LRU_HEADS, LRU_HEAD_DIM, LRU_HEAD_DIM), LRU_HEAD_DIM ** -0.5),
        'lru_b_a': nrm(ks[11], (DEPTH, D_LRU), 0.01),
        'lru_w_x': nrm(ks[12], (DEPTH, LRU_HEADS, LRU_HEAD_DIM, LRU_HEAD_DIM), LRU_HEAD_DIM ** -0.5),
        'lru_b_x': nrm(ks[13], (DEPTH, D_LRU), 0.01),
        'lru_lambda': lam,
        'conv_sc_w': nrm(ks[14], (DEPTH, SC_CONV, D_SC), SC_CONV ** -0.5),
        'g_out_lru': gain(ks[15], (DEPTH, D_LRU)),
        'g_out_sc': gain(ks[16], (DEPTH, D_SC)),
        'w_out': nrm(ks[17], (DEPTH, D_MIX, D_MODEL), D_MIX ** -0.5),
        'g_ffn': gain(ks[18], (DEPTH, D_MODEL)),
        'peer_w_q': nrm(ks[19], (DEPTH, D_MODEL, PEER_HEADS * D_KEY), D_MODEL ** -0.5),
        'peer_k1': nrm(ks[20], (DEPTH, N_KEYS, D_HALF), D_HALF ** -0.5),
        'peer_k2': nrm(ks[21], (DEPTH, N_KEYS, D_HALF), D_HALF ** -0.5),
        'peer_u': nrm(ks[22], (DEPTH, N_EXPERTS, D_MODEL), D_MODEL ** -0.5),
        'peer_v': nrm(ks[23], (DEPTH, N_EXPERTS, D_MODEL), PEER_HEADS ** -0.5),
        'g_final': gain(ks[24], (D_MODEL,)),
    }


def reference(x_prompt, x_sample, state_lru, cache_conv_lru, cache_conv_short, g_mix, w_in, conv_lru_w,
              conv_lru_b, lru_w_a, lru_b_a, lru_w_x, lru_b_x, lru_lambda, conv_sc_w, g_out_lru, g_out_sc,
              w_out, g_ffn, peer_w_q, peer_k1, peer_k2, peer_u, peer_v, g_final):
    def run_layer(x, h0, c4, c3, l, reset_first):
        mix, h_new, c4_new, c3_new = token_mixers(
            rmsnorm(x, g_mix[l]), h0, c4, c3, w_in[l], conv_lru_w[l], conv_lru_b[l], lru_w_a[l], lru_b_a[l],
            lru_w_x[l], lru_b_x[l], lru_lambda[l], conv_sc_w[l], g_out_lru[l], g_out_sc[l], w_out[l], reset_first)
        x = x + mix
        x = x + peer(rmsnorm(x, g_ffn[l]), peer_w_q[l], peer_k1[l], peer_k2[l], peer_u[l], peer_v[l])
        return x, h_new, c4_new, c3_new

    dt = x_prompt.dtype
    xp, xs = x_prompt, x_sample
    hp_l, c4p_l, c3p_l, hs_l, c4s_l, c3s_l = [], [], [], [], [], []
    for l in range(DEPTH):
        xp, hp, c4p, c3p = run_layer(
            xp, jnp.zeros((BATCH, D_LRU), dt), jnp.zeros((BATCH, LRU_CONV - 1, D_LRU), dt),
            jnp.zeros((BATCH, SC_CONV - 1, D_SC), dt), l, True)
        xs, hs, c4s, c3s = run_layer(xs, state_lru[l], cache_conv_lru[l], cache_conv_short[l], l, False)
        hp_l.append(hp); c4p_l.append(c4p); c3p_l.append(c3p)
        hs_l.append(hs); c4s_l.append(c4s); c3s_l.append(c3s)
    y_prompt = rmsnorm(xp, g_final)
    y_sample = rmsnorm(xs, g_final)
    return (y_prompt, y_sample, jnp.stack(hp_l), jnp.stack(c4p_l), jnp.stack(c3p_l),
            jnp.stack(hs_l), jnp.stack(c4s_l), jnp.stack(c3s_l))
```

```python
import functools

import jax
import jax.numpy as jnp
from jax import lax
from jax.experimental import pallas as pl
from jax.experimental.pallas import tpu as pltpu

D_MODEL = 2048
D_LRU = 1024
D_SC = 1024
D_IN = 2 * D_LRU + 3 * D_SC
LRU_HEADS = 16
LRU_HEAD_DIM = D_LRU // LRU_HEADS
LRU_CONV = 4
SC_CONV = 3
LRU_C = 8.0
N_KEYS = 128
N_EXPERTS = N_KEYS * N_KEYS
PEER_HEADS = 8
PEER_TOPK = 16
D_HALF = 128
EPS = 1e-6

LANES = 128
SUBLANES = 8
VMEM_LIMIT = 56 * 1024 * 1024

F32 = jnp.float32
BF16 = jnp.bfloat16
NEG_INF = float("-inf")


def _rms(x, g):
    ms = jnp.mean(x * x, axis=-1, keepdims=True)
    return x * lax.rsqrt(ms + EPS) * g


EXPM1_SERIES_TERMS = 9
EXPM1_SERIES_BOUND = 0.35


def _neg_expm1(x, exp_x):
    s = 1.0 + x * (1.0 / EXPM1_SERIES_TERMS)
    for n in range(EXPM1_SERIES_TERMS - 1, 1, -1):
        s = 1.0 + (x * (1.0 / n)) * s
    return jnp.where(x > -EXPM1_SERIES_BOUND, -x * s, 1.0 - exp_x)


def _nt_dot(a, b):
    return lax.dot_general(a, b, (((1,), (1,)), ((), ())), preferred_element_type=F32)


def _norm_matmul_kernel(x_ref, g_ref, w_ref, o_ref, xn_ref):
    @pl.when(pl.program_id(1) == 0)
    def _():
        xn_ref[...] = _rms(x_ref[...], g_ref[...]).astype(BF16)

    o_ref[...] = jnp.dot(xn_ref[...], w_ref[...], preferred_element_type=F32)


def _norm_matmul(x, g, w, tm, tn):
    t, d = x.shape
    n = w.shape[1]
    return pl.pallas_call(
        _norm_matmul_kernel,
        grid=(t // tm, n // tn),
        in_specs=[
            pl.BlockSpec((tm, d), lambda i, j: (i, 0)),
            pl.BlockSpec((1, d), lambda i, j: (0, 0)),
            pl.BlockSpec((d, tn), lambda i, j: (0, j)),
        ],
        out_specs=pl.BlockSpec((tm, tn), lambda i, j: (i, j)),
        out_shape=jax.ShapeDtypeStruct((t, n), F32),
        scratch_shapes=[pltpu.VMEM((tm, d), BF16)],
        compiler_params=pltpu.CompilerParams(
            dimension_semantics=("arbitrary", "arbitrary"), vmem_limit_bytes=VMEM_LIMIT),
        name="norm_in_proj",
    )(x, g, w)


def _mixer_kernel(z_ref, h0_ref, c4_ref, c3_ref, cw_ref, cb_ref, wg_ref, ba_ref, bx_ref, lam_ref,
                  sw_ref, gl_ref, gs_ref, o_ref, tail_ref,
                  xbuf, ubuf, a_s, b_s, h_s, hcar, *, tt, reset_first):
    t = pl.program_id(1)

    @pl.when(t == 0)
    def _():
        xbuf[0:SUBLANES, :] = c4_ref[0]
        ubuf[0:SUBLANES, :] = c3_ref[0]
        hcar[...] = jnp.broadcast_to(h0_ref[0], (SUBLANES, D_LRU))

    @pl.when(t > 0)
    def _():
        xbuf[0:SUBLANES, :] = xbuf[tt:tt + SUBLANES, :]
        ubuf[0:SUBLANES, :] = ubuf[tt:tt + SUBLANES, :]

    xbuf[SUBLANES:SUBLANES + tt, :] = z_ref[:, 0:D_LRU]
    ubuf[SUBLANES:SUBLANES + tt, :] = (z_ref[:, 2 * D_LRU + D_SC:2 * D_LRU + 2 * D_SC]
                                       * z_ref[:, 2 * D_LRU + 2 * D_SC:D_IN])

    nlam = -lam_ref[...]
    sp = jnp.maximum(nlam, 0.0) + jnp.log1p(jnp.exp(-jnp.abs(nlam)))

    row = lax.broadcasted_iota(jnp.int32, (tt, LANES), 0)
    for j in range(D_LRU // LANES):
        sl = slice(LANES * j, LANES * (j + 1))
        base = SUBLANES - (LRU_CONV - 1)
        xc = xbuf[base:base + tt, sl] * cw_ref[0:1, sl]
        for k in range(1, LRU_CONV):
            xc = xc + xbuf[base + k:base + k + tt, sl] * cw_ref[k:k + 1, sl]
        xc = xc + cb_ref[:, sl]
        gates = jnp.dot(xc.astype(BF16), wg_ref[j], preferred_element_type=F32)
        r = jax.nn.sigmoid(gates[:, 0:LANES] + ba_ref[:, sl])
        i = jax.nn.sigmoid(gates[:, LANES:2 * LANES] + bx_ref[:, sl])
        log_a = (-LRU_C * r) * sp[:, sl]
        a = jnp.exp(log_a)
        mult = jnp.sqrt(_neg_expm1(2.0 * log_a, a * a))
        if reset_first:
            mult = jnp.where(jnp.logical_and(row == 0, t == 0), 1.0, mult)
        a_s[:, sl] = a
        b_s[:, sl] = mult * (i * xc)

    srow = lax.broadcasted_iota(jnp.int32, (SUBLANES, D_LRU), 0)

    def group(gi, carry):
        r0 = pl.multiple_of(gi * SUBLANES, SUBLANES)
        av = a_s[pl.ds(r0, SUBLANES), :]
        bv = b_s[pl.ds(r0, SUBLANES), :]
        for d in (1, 2, 4):
            keep = srow >= d
            a_sh = jnp.where(keep, pltpu.roll(av, d, 0), 1.0)
            b_sh = jnp.where(keep, pltpu.roll(bv, d, 0), 0.0)
            bv = av * b_sh + bv
            av = av * a_sh
        hprev = jnp.broadcast_to(hcar[SUBLANES - 1:SUBLANES, :], (SUBLANES, D_LRU))
        hv = bv + av * hprev
        h_s[pl.ds(r0, SUBLANES), :] = hv
        hcar[...] = hv
        return carry

    lax.fori_loop(0, tt // SUBLANES, group, 0)

    rc = min(tt, 32)
    for c in range(tt // rc):
        rs = slice(rc * c, rc * (c + 1))
        o_lru = h_s[rs, :] * jax.nn.gelu(z_ref[rs, D_LRU:2 * D_LRU])
        o_ref[rs, 0:D_LRU] = _rms(o_lru, gl_ref[...]).astype(BF16)
        base = SUBLANES - (SC_CONV - 1) + rc * c
        uc = ubuf[base:base + rc, :] * sw_ref[0:1, :]
        for k in range(1, SC_CONV):
            uc = uc + ubuf[base + k:base + k + rc, :] * sw_ref[k:k + 1, :]
        o_sc = z_ref[rs, 2 * D_LRU:2 * D_LRU + D_SC] * uc
        o_ref[rs, D_LRU:D_LRU + D_SC] = _rms(o_sc, gs_ref[...]).astype(BF16)

    tail_ref[0, 0] = h_s[tt - SUBLANES:tt, :]
    tail_ref[0, 1] = xbuf[tt:tt + SUBLANES, :]
    tail_ref[0, 2] = ubuf[tt:tt + SUBLANES, :]


def _mixer(z, row0, nseq, tseq, tt, h0, c4, c3, params, reset_first):
    nt = tseq // tt
    blk0 = row0 // tt
    cw, cb, wg, ba, bx, lam, sw, gl, gs = params
    full = lambda a: pl.BlockSpec(a.shape, lambda s, t: (0,) * a.ndim)
    return pl.pallas_call(
        functools.partial(_mixer_kernel, tt=tt, reset_first=reset_first),
        grid=(nseq, nt),
        in_specs=[
            pl.BlockSpec((tt, D_IN), lambda s, t: (blk0 + s * nt + t, 0)),
            pl.BlockSpec((1, 1, D_LRU), lambda s, t: (s, 0, 0)),
            pl.BlockSpec((1, SUBLANES, D_LRU), lambda s, t: (s, 0, 0)),
            pl.BlockSpec((1, SUBLANES, D_SC), lambda s, t: (s, 0, 0)),
            full(cw), full(cb), full(wg), full(ba), full(bx), full(lam), full(sw), full(gl), full(gs),
        ],
        out_specs=[
            pl.BlockSpec((tt, D_LRU + D_SC), lambda s, t: (s * nt + t, 0)),
            pl.BlockSpec((1, 3, SUBLANES, D_LRU), lambda s, t: (s, 0, 0, 0)),
        ],
        out_shape=[
            jax.ShapeDtypeStruct((nseq * tseq, D_LRU + D_SC), BF16),
            jax.ShapeDtypeStruct((nseq, 3, SUBLANES, D_LRU), F32),
        ],
        scratch_shapes=[
            pltpu.VMEM((tt + 2 * SUBLANES, D_LRU), F32),
            pltpu.VMEM((tt + 2 * SUBLANES, D_SC), F32),
            pltpu.VMEM((tt, D_LRU), F32),
            pltpu.VMEM((tt, D_LRU), F32),
            pltpu.VMEM((tt, D_LRU), F32),
            pltpu.VMEM((SUBLANES, D_LRU), F32),
        ],
        compiler_params=pltpu.CompilerParams(
            dimension_semantics=("arbitrary", "arbitrary"), vmem_limit_bytes=VMEM_LIMIT),
        name="mixer_reset" if reset_first else "mixer_cont",
    )(z, h0, c4, c3, cw, cb, wg, ba, bx, lam, sw, gl, gs)


def _out_q_kernel(o_ref, x_ref, wout_ref, g_ref, wq_ref, x1_ref, xn_ref, q_ref):
    x1 = x_ref[...] + jnp.dot(o_ref[...], wout_ref[...], preferred_element_type=F32)
    x1_ref[...] = x1
    xn = _rms(x1, g_ref[...]).astype(BF16)
    xn_ref[...] = xn
    q = jnp.dot(xn, wq_ref[...], preferred_element_type=F32).astype(BF16)
    for c in range(2 * PEER_HEADS):
        q_ref[c] = q[:, D_HALF * c:D_HALF * (c + 1)]


def _out_q(o, x, w_out, g, w_q, tm):
    t = x.shape[0]
    const = lambda a: pl.BlockSpec(a.shape, lambda i: (0,) * a.ndim, pipeline_mode=pl.Buffered(1))
    return pl.pallas_call(
        _out_q_kernel,
        grid=(t // tm,),
        in_specs=[
            pl.BlockSpec((tm, D_MODEL), lambda i: (i, 0)),
            pl.BlockSpec((tm, D_MODEL), lambda i: (i, 0)),
            const(w_out), const(g), const(w_q),
        ],
        out_specs=[
            pl.BlockSpec((tm, D_MODEL), lambda i: (i, 0)),
            pl.BlockSpec((tm, D_MODEL), lambda i: (i, 0)),
            pl.BlockSpec((2 * PEER_HEADS, tm, D_HALF), lambda i: (0, i, 0)),
        ],
        out_shape=[
            jax.ShapeDtypeStruct((t, D_MODEL), F32),
            jax.ShapeDtypeStruct((t, D_MODEL), BF16),
            jax.ShapeDtypeStruct((2 * PEER_HEADS, t, D_HALF), BF16),
        ],
        compiler_params=pltpu.CompilerParams(
            dimension_semantics=("arbitrary",), vmem_limit_bytes=VMEM_LIMIT),
        name="out_proj_q_proj",
    )(o, x, w_out, g, w_q)


N_RANK = PEER_TOPK + 1


def _top_values(s, n):
    vals = [jnp.max(s, axis=0, keepdims=True)]
    for _ in range(n - 1):
        vals.append(jnp.max(jnp.where(s < vals[-1], s, NEG_INF), axis=0, keepdims=True))
    return vals


def _peer_select_kernel(q1_ref, q2_ref, k1_ref, k2_ref, thr_ref, e1_ref, s2_ref, e2_ref, *, tb):
    s1_all = _nt_dot(k1_ref[...], q1_ref[0])
    s2_all = _nt_dot(k2_ref[...], q2_ref[0])
    for l in range(tb // LANES):
        sl = slice(LANES * l, LANES * (l + 1))
        s1 = s1_all[:, sl]
        s2 = s2_all[:, sl]
        v1 = _top_values(s1, N_RANK)
        v2 = _top_values(s2, N_RANK)
        pad = [jnp.full((1, LANES), NEG_INF, F32)] * (3 * SUBLANES - N_RANK)
        v2_col = jnp.concatenate(v2 + pad, axis=0)
        cands = [v1[0] + v2_col] + [v1[a] + v2_col[0:SUBLANES] for a in range(1, N_RANK)]
        cand = jnp.concatenate(cands, axis=0)
        c = _top_values(cand, N_RANK)
        theta = 0.5 * (c[PEER_TOPK - 1] + c[PEER_TOPK])
        z = jnp.sum(jnp.where(cand >= theta, jnp.exp(cand - c[0]), 0.0), axis=0, keepdims=True)
        thr_ref[0, :, sl] = theta - s1
        e1_ref[0, :, sl] = jnp.exp(s1 - v1[0]) / z
        s2_ref[0, :, sl] = s2
        e2_ref[0, :, sl] = jnp.exp(s2 - v2[0])


def _peer_select(q, k1, k2, tb):
    t = q.shape[1]
    out = jax.ShapeDtypeStruct((PEER_HEADS, N_KEYS, t), F32)
    out_spec = pl.BlockSpec((1, N_KEYS, tb), lambda i, h: (h, 0, i))
    return pl.pallas_call(
        functools.partial(_peer_select_kernel, tb=tb),
        grid=(t // tb, PEER_HEADS),
        in_specs=[
            pl.BlockSpec((1, tb, D_HALF), lambda i, h: (2 * h, i, 0)),
            pl.BlockSpec((1, tb, D_HALF), lambda i, h: (2 * h + 1, i, 0)),
            pl.BlockSpec((N_KEYS, D_HALF), lambda i, h: (0, 0)),
            pl.BlockSpec((N_KEYS, D_HALF), lambda i, h: (0, 0)),
        ],
        out_specs=[out_spec] * 4,
        out_shape=[out] * 4,
        compiler_params=pltpu.CompilerParams(
            dimension_semantics=("arbitrary", "arbitrary"), vmem_limit_bytes=VMEM_LIMIT),
        name="peer_select",
    )(q, q, k1, k2)


def _peer_dense_kernel(u_ref, vt_ref, xn_ref, x1_ref, thr_ref, e1_ref, s2_ref, e2_ref, g_ref, y_ref,
                       act_ref, wt_ref, acc_ref, *, tb, eb):
    e = pl.program_id(1)

    @pl.when(e == 0)
    def _():
        acc_ref[...] = jnp.zeros_like(acc_ref)

    act_ref[...] = _nt_dot(u_ref[...], xn_ref[...])

    assert eb // N_KEYS == SUBLANES
    i0 = pl.multiple_of(e * SUBLANES, SUBLANES)
    for l in range(tb // LANES):
        sl = slice(LANES * l, LANES * (l + 1))
        thr8 = [thr_ref[h, pl.ds(i0, SUBLANES), sl] for h in range(PEER_HEADS)]
        e18 = [e1_ref[h, pl.ds(i0, SUBLANES), sl] for h in range(PEER_HEADS)]
        for ib in range(SUBLANES):
            rows = slice(N_KEYS * ib, N_KEYS * (ib + 1))
            w = jnp.zeros((N_KEYS, LANES), F32)
            for h in range(PEER_HEADS):
                thr = thr8[h][ib:ib + 1]
                e1 = e18[h][ib:ib + 1]
                w = w + jnp.where(s2_ref[h, :, sl] >= thr, e1 * e2_ref[h, :, sl], 0.0)
            wt_ref[rows, sl] = (w * jax.nn.gelu(act_ref[rows, sl])).astype(BF16)

    acc_ref[...] += jnp.dot(vt_ref[...], wt_ref[...], preferred_element_type=F32)

    @pl.when(e == pl.num_programs(1) - 1)
    def _():
        x2 = x1_ref[...] + acc_ref[...].T
        y_ref[...] = _rms(x2, g_ref[...])


def _peer_dense(u, vt, xn, x1, thr, e1, s2, e2, g, tb, eb):
    t = xn.shape[0]
    tok = lambda a: pl.BlockSpec((tb, a.shape[1]), lambda i, e: (i, 0), pipeline_mode=pl.Buffered(1))
    sel = pl.BlockSpec((PEER_HEADS, N_KEYS, tb), lambda i, e: (0, 0, i), pipeline_mode=pl.Buffered(1))
    return pl.pallas_call(
        functools.partial(_peer_dense_kernel, tb=tb, eb=eb),
        grid=(t // tb, N_EXPERTS // eb),
        in_specs=[
            pl.BlockSpec((eb, D_MODEL), lambda i, e: (e, 0)),
            pl.BlockSpec((D_MODEL, eb), lambda i, e: (0, e)),
            tok(xn), tok(x1), sel, sel, sel, sel,
            pl.BlockSpec((1, D_MODEL), lambda i, e: (0, 0)),
        ],
        out_specs=pl.BlockSpec((tb, D_MODEL), lambda i, e: (i, 0)),
        out_shape=jax.ShapeDtypeStruct((t, D_MODEL), F32),
        scratch_shapes=[
            pltpu.VMEM((eb, tb), F32),
            pltpu.VMEM((eb, tb), BF16),
            pltpu.VMEM((D_MODEL, tb), F32),
        ],
        compiler_params=pltpu.CompilerParams(
            dimension_semantics=("arbitrary", "arbitrary"), vmem_limit_bytes=VMEM_LIMIT),
        name="peer_dense",
    )(u, vt, xn, x1, thr, e1, s2, e2, g)


def _gate_weights(w_a, w_x):
    def blockdiag(w):
        w = w.reshape(LRU_HEADS // 2, 2, LRU_HEAD_DIM, LRU_HEAD_DIM)
        zero = jnp.zeros_like(w[:, 0])
        top = jnp.concatenate([w[:, 0], zero], axis=-1)
        bot = jnp.concatenate([zero, w[:, 1]], axis=-1)
        return jnp.concatenate([top, bot], axis=-2)
    return jnp.concatenate([blockdiag(w_a), blockdiag(w_x)], axis=-1).astype(BF16)


def _right_align(cache, rows):
    n, w, c = cache.shape
    return jnp.concatenate([jnp.zeros((n, rows - w, c), cache.dtype), cache], axis=1)


def kernel(x_prompt, x_sample, state_lru, cache_conv_lru, cache_conv_short, g_mix, w_in, conv_lru_w,
           conv_lru_b, lru_w_a, lru_b_a, lru_w_x, lru_b_x, lru_lambda, conv_sc_w, g_out_lru, g_out_sc,
           w_out, g_ffn, peer_w_q, peer_k1, peer_k2, peer_u, peer_v, g_final):
    depth = g_mix.shape[0]
    assert depth == 1, "the final rmsnorm is fused into the (single) layer's PEER kernel"
    nb, seq, _ = x_prompt.shape
    ndec, dseq, _ = x_sample.shape
    n_prompt = nb * seq
    x = jnp.concatenate([x_prompt.reshape(n_prompt, D_MODEL), x_sample.reshape(ndec * dseq, D_MODEL)], axis=0)

    outs = {k: [] for k in ("hp", "c4p", "c3p", "hs", "c4s", "c3s")}
    for l in range(depth):
        row = lambda a: a[l].reshape(1, -1)
        z = _norm_matmul(x, row(g_mix), w_in[l].astype(BF16), tm=512, tn=1024)

        params = (conv_lru_w[l], row(conv_lru_b), _gate_weights(lru_w_a[l], lru_w_x[l]), row(lru_b_a),
                  row(lru_b_x), row(lru_lambda), conv_sc_w[l], row(g_out_lru), row(g_out_sc))
        o_p, tail_p = _mixer(
            z, 0, nb, seq, 256, jnp.zeros((nb, 1, D_LRU), F32), jnp.zeros((nb, SUBLANES, D_LRU), F32),
            jnp.zeros((nb, SUBLANES, D_SC), F32), params, True)
        o_s, tail_s = _mixer(
            z, n_prompt, ndec, dseq, dseq, state_lru[l].reshape(ndec, 1, D_LRU),
            _right_align(cache_conv_lru[l], SUBLANES), _right_align(cache_conv_short[l], SUBLANES),
            params, False)
        o = jnp.concatenate([o_p, o_s], axis=0)

        x1, xn2, q = _out_q(o, x, w_out[l].astype(BF16), row(g_ffn), peer_w_q[l].astype(BF16), tm=256)
        thr, e1, s2, e2 = _peer_select(q, peer_k1[l].astype(BF16), peer_k2[l].astype(BF16), tb=512)
        x = _peer_dense(peer_u[l].astype(BF16), peer_v[l].T.astype(BF16), xn2, x1, thr, e1, s2, e2,
                        g_final.reshape(1, -1), tb=512, eb=SUBLANES * N_KEYS)

        for name, tail, i, w in (("hp", tail_p, 0, 1), ("c4p", tail_p, 1, LRU_CONV - 1),
                                 ("c3p", tail_p, 2, SC_CONV - 1), ("hs", tail_s, 0, 1),
                                 ("c4s", tail_s, 1, LRU_CONV - 1), ("c3s", tail_s, 2, SC_CONV - 1)):
            outs[name].append(tail[:, i, SUBLANES - w:, :])

    y_prompt = x[:n_prompt].reshape(x_prompt.shape)
    y_sample = x[n_prompt:].reshape(x_sample.shape)
    hp = jnp.stack(outs["hp"])[:, :, 0, :]
    hs = jnp.stack(outs["hs"])[:, :, 0, :]
    return (y_prompt, y_sample, hp, jnp.stack(outs["c4p"]), jnp.stack(outs["c3p"]),
            hs, jnp.stack(outs["c4s"]), jnp.stack(outs["c3s"]))
```

```python
import functools

import jax
import jax.numpy as jnp
from jax import lax
from jax.experimental import pallas as pl
from jax.experimental.pallas import tpu as pltpu

D_MODEL = 2048
D_LRU = 1024
D_SC = 1024
D_IN = 2 * D_LRU + 3 * D_SC
LRU_HEADS = 16
LRU_HEAD_DIM = D_LRU // LRU_HEADS
LRU_CONV = 4
SC_CONV = 3
LRU_C = 8.0
N_KEYS = 128
N_EXPERTS = N_KEYS * N_KEYS
PEER_HEADS = 8
PEER_TOPK = 16
D_HALF = 128
EPS = 1e-6

LANES = 128
SUBLANES = 8
VMEM_LIMIT = 56 * 1024 * 1024

F32 = jnp.float32
BF16 = jnp.bfloat16
NEG_INF = float("-inf")


def _rms(x, g):
    ms = jnp.mean(x * x, axis=-1, keepdims=True)
    return x * lax.rsqrt(ms + EPS) * g


EXPM1_SERIES_TERMS = 9
EXPM1_SERIES_BOUND = 0.35


def _neg_expm1(x, exp_x):
    s = 1.0 + x * (1.0 / EXPM1_SERIES_TERMS)
    for n in range(EXPM1_SERIES_TERMS - 1, 1, -1):
        s = 1.0 + (x * (1.0 / n)) * s
    return jnp.where(x > -EXPM1_SERIES_BOUND, -x * s, 1.0 - exp_x)


def _nt_dot(a, b):
    return lax.dot_general(a, b, (((1,), (1,)), ((), ())), preferred_element_type=F32)


def _norm_matmul_kernel(x_ref, g_ref, w_ref, o_ref, xn_ref):
    @pl.when(pl.program_id(1) == 0)
    def _():
        xn_ref[...] = _rms(x_ref[...], g_ref[...]).astype(BF16)

    o_ref[...] = jnp.dot(xn_ref[...], w_ref[...], preferred_element_type=F32)


def _norm_matmul(x, g, w, tm, tn):
    t, d = x.shape
    n = w.shape[1]
    return pl.pallas_call(
        _norm_matmul_kernel,
        grid=(t // tm, n // tn),
        in_specs=[
            pl.BlockSpec((tm, d), lambda i, j: (i, 0)),
            pl.BlockSpec((1, d), lambda i, j: (0, 0)),
            pl.BlockSpec((d, tn), lambda i, j: (0, j)),
        ],
        out_specs=pl.BlockSpec((tm, tn), lambda i, j: (i, j)),
        out_shape=jax.ShapeDtypeStruct((t, n), F32),
        scratch_shapes=[pltpu.VMEM((tm, d), BF16)],
        compiler_params=pltpu.CompilerParams(
            dimension_semantics=("arbitrary", "arbitrary"), vmem_limit_bytes=VMEM_LIMIT),
        name="norm_in_proj",
    )(x, g, w)


def _mixer_kernel(z_ref, h0_ref, c4_ref, c3_ref, cw_ref, cb_ref, wg_ref, ba_ref, bx_ref, lam_ref,
                  sw_ref, gl_ref, gs_ref, o_ref, tail_ref,
                  xbuf, ubuf, a_s, b_s, h_s, hcar, *, tt, reset_first):
    t = pl.program_id(1)

    @pl.when(t == 0)
    def _():
        xbuf[0:SUBLANES, :] = c4_ref[0]
        ubuf[0:SUBLANES, :] = c3_ref[0]
        hcar[...] = jnp.broadcast_to(h0_ref[0], (SUBLANES, D_LRU))

    @pl.when(t > 0)
    def _():
        xbuf[0:SUBLANES, :] = xbuf[tt:tt + SUBLANES, :]
        ubuf[0:SUBLANES, :] = ubuf[tt:tt + SUBLANES, :]

    xbuf[SUBLANES:SUBLANES + tt, :] = z_ref[:, 0:D_LRU]
    ubuf[SUBLANES:SUBLANES + tt, :] = (z_ref[:, 2 * D_LRU + D_SC:2 * D_LRU + 2 * D_SC]
                                       * z_ref[:, 2 * D_LRU + 2 * D_SC:D_IN])

    nlam = -lam_ref[...]
    sp = jnp.maximum(nlam, 0.0) + jnp.log1p(jnp.exp(-jnp.abs(nlam)))

    row = lax.broadcasted_iota(jnp.int32, (tt, LANES), 0)
    for j in range(D_LRU // LANES):
        sl = slice(LANES * j, LANES * (j + 1))
        base = SUBLANES - (LRU_CONV - 1)
        xc = xbuf[base:base + tt, sl] * cw_ref[0:1, sl]
        for k in range(1, LRU_CONV):
            xc = xc + xbuf[base + k:base + k + tt, sl] * cw_ref[k:k + 1, sl]
        xc = xc + cb_ref[:, sl]
        gates = jnp.dot(xc.astype(BF16), wg_ref[j], preferred_element_type=F32)
        r = jax.nn.sigmoid(gates[:, 0:LANES] + ba_ref[:, sl])
        i = jax.nn.sigmoid(gates[:, LANES:2 * LANES] + bx_ref[:, sl])
        log_a = (-LRU_C * r) * sp[:, sl]
        a = jnp.exp(log_a)
        mult = jnp.sqrt(_neg_expm1(2.0 * log_a, a * a))
        if reset_first:
            mult = jnp.where(jnp.logical_and(row == 0, t == 0), 1.0, mult)
        a_s[:, sl] = a
        b_s[:, sl] = mult * (i * xc)

    srow = lax.broadcasted_iota(jnp.int32, (SUBLANES, D_LRU), 0)

    def group(gi, carry):
        r0 = pl.multiple_of(gi * SUBLANES, SUBLANES)
        av = a_s[pl.ds(r0, SUBLANES), :]
        bv = b_s[pl.ds(r0, SUBLANES), :]
        for d in (1, 2, 4):
            keep = srow >= d
            a_sh = jnp.where(keep, pltpu.roll(av, d, 0), 1.0)
            b_sh = jnp.where(keep, pltpu.roll(bv, d, 0), 0.0)
            bv = av * b_sh + bv
            av = av * a_sh
        hprev = jnp.broadcast_to(hcar[SUBLANES - 1:SUBLANES, :], (SUBLANES, D_LRU))
        hv = bv + av * hprev
        h_s[pl.ds(r0, SUBLANES), :] = hv
        hcar[...] = hv
        return carry

    lax.fori_loop(0, tt // SUBLANES, group, 0)

    rc = min(tt, 32)
    for c in range(tt // rc):
        rs = slice(rc * c, rc * (c + 1))
        o_lru = h_s[rs, :] * jax.nn.gelu(z_ref[rs, D_LRU:2 * D_LRU])
        o_ref[rs, 0:D_LRU] = _rms(o_lru, gl_ref[...]).astype(BF16)
        base = SUBLANES - (SC_CONV - 1) + rc * c
        uc = ubuf[base:base + rc, :] * sw_ref[0:1, :]
        for k in range(1, SC_CONV):
            uc = uc + ubuf[base + k:base + k + rc, :] * sw_ref[k:k + 1, :]
        o_sc = z_ref[rs, 2 * D_LRU:2 * D_LRU + D_SC] * uc
        o_ref[rs, D_LRU:D_LRU + D_SC] = _rms(o_sc, gs_ref[...]).astype(BF16)

    tail_ref[0, 0] = h_s[tt - SUBLANES:tt, :]
    tail_ref[0, 1] = xbuf[tt:tt + SUBLANES, :]
    tail_ref[0, 2] = ubuf[tt:tt + SUBLANES, :]


def _mixer(z, row0, nseq, tseq, tt, h0, c4, c3, params, reset_first):
    nt = tseq // tt
    blk0 = row0 // tt
    cw, cb, wg, ba, bx, lam, sw, gl, gs = params
    full = lambda a: pl.BlockSpec(a.shape, lambda s, t: (0,) * a.ndim)
    return pl.pallas_call(
        functools.partial(_mixer_kernel, tt=tt, reset_first=reset_first),
        grid=(nseq, nt),
        in_specs=[
            pl.BlockSpec((tt, D_IN), lambda s, t: (blk0 + s * nt + t, 0)),
            pl.BlockSpec((1, 1, D_LRU), lambda s, t: (s, 0, 0)),
            pl.BlockSpec((1, SUBLANES, D_LRU), lambda s, t: (s, 0, 0)),
            pl.BlockSpec((1, SUBLANES, D_SC), lambda s, t: (s, 0, 0)),
            full(cw), full(cb), full(wg), full(ba), full(bx), full(lam), full(sw), full(gl), full(gs),
        ],
        out_specs=[
            pl.BlockSpec((tt, D_LRU + D_SC), lambda s, t: (s * nt + t, 0)),
            pl.BlockSpec((1, 3, SUBLANES, D_LRU), lambda s, t: (s, 0, 0, 0)),
        ],
        out_shape=[
            jax.ShapeDtypeStruct((nseq * tseq, D_LRU + D_SC), BF16),
            jax.ShapeDtypeStruct((nseq, 3, SUBLANES, D_LRU), F32),
        ],
        scratch_shapes=[
            pltpu.VMEM((tt + 2 * SUBLANES, D_LRU), F32),
            pltpu.VMEM((tt + 2 * SUBLANES, D_SC), F32),
            pltpu.VMEM((tt, D_LRU), F32),
            pltpu.VMEM((tt, D_LRU), F32),
            pltpu.VMEM((tt, D_LRU), F32),
            pltpu.VMEM((SUBLANES, D_LRU), F32),
        ],
        compiler_params=pltpu.CompilerParams(
            dimension_semantics=("arbitrary", "arbitrary"), vmem_limit_bytes=VMEM_LIMIT),
        name="mixer_reset" if reset_first else "mixer_cont",
    )(z, h0, c4, c3, cw, cb, wg, ba, bx, lam, sw, gl, gs)


def _out_q_kernel(o_ref, x_ref, wout_ref, g_ref, wq_ref, x1_ref, xnt_ref, q_ref):
    x1 = x_ref[...] + jnp.dot(o_ref[...], wout_ref[...], preferred_element_type=F32)
    x1_ref[...] = x1
    xn32 = _rms(x1, g_ref[...])
    xn = xn32.astype(BF16)
    xnt_ref[...] = xn32.T.astype(BF16)
    q = jnp.dot(xn, wq_ref[...], preferred_element_type=F32).astype(BF16)
    for c in range(2 * PEER_HEADS):
        q_ref[c] = q[:, D_HALF * c:D_HALF * (c + 1)]


def _out_q(o, x, w_out, g, w_q, tm):
    t = x.shape[0]
    const = lambda a: pl.BlockSpec(a.shape, lambda i: (0,) * a.ndim, pipeline_mode=pl.Buffered(1))
    return pl.pallas_call(
        _out_q_kernel,
        grid=(t // tm,),
        in_specs=[
            pl.BlockSpec((tm, D_MODEL), lambda i: (i, 0)),
            pl.BlockSpec((tm, D_MODEL), lambda i: (i, 0)),
            const(w_out), const(g), const(w_q),
        ],
        out_specs=[
            pl.BlockSpec((tm, D_MODEL), lambda i: (i, 0)),
            pl.BlockSpec((D_MODEL, tm), lambda i: (0, i)),
            pl.BlockSpec((2 * PEER_HEADS, tm, D_HALF), lambda i: (0, i, 0)),
        ],
        out_shape=[
            jax.ShapeDtypeStruct((t, D_MODEL), F32),
            jax.ShapeDtypeStruct((D_MODEL, t), BF16),
            jax.ShapeDtypeStruct((2 * PEER_HEADS, t, D_HALF), BF16),
        ],
        compiler_params=pltpu.CompilerParams(
            dimension_semantics=("arbitrary",), vmem_limit_bytes=VMEM_LIMIT),
        name="out_proj_q_proj",
    )(o, x, w_out, g, w_q)


N_RANK = PEER_TOPK + 1
BF16_ROWS = 2 * SUBLANES
RANK_SCALE = 2.0 ** 20
KEY_GROUPS = N_KEYS // BF16_ROWS


def _top_values(s, n):
    vals = [jnp.max(s, axis=0, keepdims=True)]
    for _ in range(n - 1):
        vals.append(jnp.max(jnp.where(s < vals[-1], s, NEG_INF), axis=0, keepdims=True))
    return vals


def _peer_select_kernel(q1_ref, q2_ref, k1_ref, k2_ref, cnt_ref, e1_ref, rank_ref, e2_ref, *, tb):
    s1_all = _nt_dot(k1_ref[...], q1_ref[0])
    s2_all = _nt_dot(k2_ref[...], q2_ref[0])
    for l in range(tb // LANES):
        sl = slice(LANES * l, LANES * (l + 1))
        s1 = s1_all[:, sl]
        s2 = s2_all[:, sl]
        v1 = _top_values(s1, N_RANK)
        v2 = _top_values(s2, N_RANK)
        pad = [jnp.full((1, LANES), NEG_INF, F32)] * (3 * SUBLANES - N_RANK)
        v2_col = jnp.concatenate(v2 + pad, axis=0)
        cands = [v1[0] + v2_col] + [v1[a] + v2_col[0:SUBLANES] for a in range(1, N_RANK)]
        cand = jnp.concatenate(cands, axis=0)
        c = _top_values(cand, N_RANK)
        theta = 0.5 * (c[PEER_TOPK - 1] + c[PEER_TOPK])
        z = jnp.sum(jnp.where(cand >= theta, jnp.exp(cand - c[0]), 0.0), axis=0, keepdims=True)
        thr = theta - s1
        cnt = jnp.zeros_like(s1)
        for b in range(PEER_TOPK):
            cnt = cnt + jnp.where(v2[b] >= thr, RANK_SCALE, 0.0)
        rank = jnp.zeros_like(s2)
        for b in range(N_RANK):
            rank = rank + jnp.where(v2[b] > s2, RANK_SCALE, 0.0)
        cnt_ref[0, :, sl] = cnt
        e1_ref[0, :, sl] = jnp.exp(s1 - v1[0]) / z
        rank_ref[0, l] = rank.astype(BF16).reshape(KEY_GROUPS, BF16_ROWS, LANES)
        e2_ref[0, l] = jnp.exp(s2 - v2[0]).astype(BF16).reshape(KEY_GROUPS, BF16_ROWS, LANES)


def _peer_select(q, k1, k2, tb):
    t = q.shape[1]
    rows = jax.ShapeDtypeStruct((PEER_HEADS, N_KEYS, t), F32)
    rows_spec = pl.BlockSpec((1, N_KEYS, tb), lambda i, h: (h, 0, i))
    tiles = jax.ShapeDtypeStruct((PEER_HEADS, t // LANES, KEY_GROUPS, BF16_ROWS, LANES), BF16)
    tiles_spec = pl.BlockSpec((1, tb // LANES, KEY_GROUPS, BF16_ROWS, LANES), lambda i, h: (h, i, 0, 0, 0))
    return pl.pallas_call(
        functools.partial(_peer_select_kernel, tb=tb),
        grid=(t // tb, PEER_HEADS),
        in_specs=[
            pl.BlockSpec((1, tb, D_HALF), lambda i, h: (2 * h, i, 0)),
            pl.BlockSpec((1, tb, D_HALF), lambda i, h: (2 * h + 1, i, 0)),
            pl.BlockSpec((N_KEYS, D_HALF), lambda i, h: (0, 0)),
            pl.BlockSpec((N_KEYS, D_HALF), lambda i, h: (0, 0)),
        ],
        out_specs=[rows_spec, rows_spec, tiles_spec, tiles_spec],
        out_shape=[rows, rows, tiles, tiles],
        compiler_params=pltpu.CompilerParams(
            dimension_semantics=("arbitrary", "arbitrary"), vmem_limit_bytes=VMEM_LIMIT),
        name="peer_select",
    )(q, q, k1, k2)


def _peer_dense_kernel(u_ref, vt_ref, xnt_ref, x1_ref, cnt_ref, e1_ref, rank_ref, e2_ref, g_ref, y_ref,
                       act0_ref, act1_ref, wt0_ref, wt1_ref, acc_ref, *, tb, eb, ne):
    s = pl.program_id(0)
    nl = tb // LANES

    @pl.when(s == 0)
    def _():
        act1_ref[...] = jnp.zeros_like(act1_ref)
        wt0_ref[...] = jnp.zeros_like(wt0_ref)
        acc_ref[...] = jnp.zeros_like(acc_ref)

    assert eb // N_KEYS == SUBLANES
    assert nl % 2 == 0 and D_MODEL // N_KEYS == 2 * SUBLANES
    pack = lambda row: jnp.broadcast_to(row, (BF16_ROWS, LANES)).astype(BF16)[None]

    def stages(p):
        act_w, act_r = (act0_ref, act1_ref) if p == 0 else (act1_ref, act0_ref)
        wt_r, wt_w = (wt0_ref, wt1_ref) if p == 0 else (wt1_ref, wt0_ref)
        piece = 0
        for half in range(nl // 2):
            cols = slice(2 * LANES * half, 2 * LANES * (half + 1))
            wt_cols = jnp.concatenate([wt_r[2 * half], wt_r[2 * half + 1]], axis=1)
            for l in (2 * half, 2 * half + 1):
                sl = slice(LANES * l, LANES * (l + 1))
                for ib in range(SUBLANES):
                    rows = slice(N_KEYS * ib, N_KEYS * (ib + 1))
                    j = piece % (2 * SUBLANES)
                    orows = slice(N_KEYS * j, N_KEYS * (j + 1))
                    acc_ref[orows, cols] += jnp.dot(vt_ref[orows, :], wt_cols, preferred_element_type=F32)
                    if piece % 2 == 0:
                        a = (piece % (2 * SUBLANES)) // 2
                        arows = slice(N_KEYS * a, N_KEYS * (a + 1))
                        strip = jnp.dot(u_ref[arows, :], xnt_ref[:, cols], preferred_element_type=F32)
                        act_w[2 * half, arows, :] = strip[:, 0:LANES]
                        act_w[2 * half + 1, arows, :] = strip[:, LANES:2 * LANES]
                    w = jnp.zeros((KEY_GROUPS, BF16_ROWS, LANES), BF16)
                    for h in range(PEER_HEADS):
                        cnt = pack(cnt_ref[h, ib:ib + 1, sl])
                        e1 = pack(e1_ref[h, ib:ib + 1, sl])
                        w = w + jnp.maximum(jnp.minimum(e1 * e2_ref[h, l], cnt - rank_ref[h, l]), 0.0)
                    g = jax.nn.gelu(act_r[l, rows, :]).astype(BF16).reshape(KEY_GROUPS, BF16_ROWS, LANES)
                    wt_w[l, rows, :] = (w * g).reshape(N_KEYS, LANES)
                    piece += 1

    for parity in range(2):
        pl.when(s % 2 == parity)(functools.partial(stages, parity))

    @pl.when(jnp.logical_and((s - 1) % ne == 0, s > 1))
    def _():
        x2 = x1_ref[...] + acc_ref[...].T
        y_ref[...] = _rms(x2, g_ref[...])
        acc_ref[...] = jnp.zeros_like(acc_ref)


def _peer_dense(u, vt, xnt, x1, cnt, e1, rank, e2, g, tb, eb):
    t = x1.shape[0]
    nt = t // tb
    ne = N_EXPERTS // eb
    tile = lambda s, lag: jnp.clip((s - lag) // ne, 0, nt - 1)
    rows_spec = pl.BlockSpec((PEER_HEADS, SUBLANES, tb), lambda s: (0, (s + ne - 1) % ne, tile(s, 1)))
    tiles_spec = pl.BlockSpec((PEER_HEADS, tb // LANES, KEY_GROUPS, BF16_ROWS, LANES),
                              lambda s: (0, tile(s, 1), 0, 0, 0))
    act_buf = pltpu.VMEM((tb // LANES, eb, LANES), F32)
    wt_buf = pltpu.VMEM((tb // LANES, eb, LANES), BF16)
    return pl.pallas_call(
        functools.partial(_peer_dense_kernel, tb=tb, eb=eb, ne=ne),
        grid=(nt * ne + 2,),
        in_specs=[
            pl.BlockSpec((eb, D_MODEL), lambda s: (s % ne, 0)),
            pl.BlockSpec((D_MODEL, eb), lambda s: (0, (s + ne - 2) % ne)),
            pl.BlockSpec((D_MODEL, tb), lambda s: (0, tile(s, 0))),
            pl.BlockSpec((tb, D_MODEL), lambda s: (tile(s, 2), 0), pipeline_mode=pl.Buffered(1)),
            rows_spec, rows_spec, tiles_spec, tiles_spec,
            pl.BlockSpec((1, D_MODEL), lambda s: (0, 0)),
        ],
        out_specs=pl.BlockSpec((tb, D_MODEL), lambda s: (tile(s, 2), 0)),
        out_shape=jax.ShapeDtypeStruct((t, D_MODEL), F32),
        scratch_shapes=[act_buf, act_buf, wt_buf, wt_buf, pltpu.VMEM((D_MODEL, tb), F32)],
        compiler_params=pltpu.CompilerParams(
            dimension_semantics=("arbitrary",), vmem_limit_bytes=VMEM_LIMIT),
        name="peer_dense",
    )(u, vt, xnt, x1, cnt, e1, rank, e2, g)


def _gate_weights(w_a, w_x):
    def blockdiag(w):
        w = w.reshape(LRU_HEADS // 2, 2, LRU_HEAD_DIM, LRU_HEAD_DIM)
        zero = jnp.zeros_like(w[:, 0])
        top = jnp.concatenate([w[:, 0], zero], axis=-1)
        bot = jnp.concatenate([zero, w[:, 1]], axis=-1)
        return jnp.concatenate([top, bot], axis=-2)
    return jnp.concatenate([blockdiag(w_a), blockdiag(w_x)], axis=-1).astype(BF16)


def _right_align(cache, rows):
    n, w, c = cache.shape
    return jnp.concatenate([jnp.zeros((n, rows - w, c), cache.dtype), cache], axis=1)


def kernel(x_prompt, x_sample, state_lru, cache_conv_lru, cache_conv_short, g_mix, w_in, conv_lru_w,
           conv_lru_b, lru_w_a, lru_b_a, lru_w_x, lru_b_x, lru_lambda, conv_sc_w, g_out_lru, g_out_sc,
           w_out, g_ffn, peer_w_q, peer_k1, peer_k2, peer_u, peer_v, g_final):
    depth = g_mix.shape[0]
    assert depth == 1, "the final rmsnorm is fused into the (single) layer's PEER kernel"
    nb, seq, _ = x_prompt.shape
    ndec, dseq, _ = x_sample.shape
    n_prompt = nb * seq
    x = jnp.concatenate([x_prompt.reshape(n_prompt, D_MODEL), x_sample.reshape(ndec * dseq, D_MODEL)], axis=0)

    outs = {k: [] for k in ("hp", "c4p", "c3p", "hs", "c4s", "c3s")}
    for l in range(depth):
        row = lambda a: a[l].reshape(1, -1)
        z = _norm_matmul(x, row(g_mix), w_in[l].astype(BF16), tm=512, tn=1024)

        params = (conv_lru_w[l], row(conv_lru_b), _gate_weights(lru_w_a[l], lru_w_x[l]), row(lru_b_a),
                  row(lru_b_x), row(lru_lambda), conv_sc_w[l], row(g_out_lru), row(g_out_sc))
        o_p, tail_p = _mixer(
            z, 0, nb, seq, 256, jnp.zeros((nb, 1, D_LRU), F32), jnp.zeros((nb, SUBLANES, D_LRU), F32),
            jnp.zeros((nb, SUBLANES, D_SC), F32), params, True)
        o_s, tail_s = _mixer(
            z, n_prompt, ndec, dseq, dseq, state_lru[l].reshape(ndec, 1, D_LRU),
            _right_align(cache_conv_lru[l], SUBLANES), _right_align(cache_conv_short[l], SUBLANES),
            params, False)
        o = jnp.concatenate([o_p, o_s], axis=0)

        x1, xnt, q = _out_q(o, x, w_out[l].astype(BF16), row(g_ffn), peer_w_q[l].astype(BF16), tm=256)
        cnt, e1, rank, e2 = _peer_select(q, peer_k1[l].astype(BF16), peer_k2[l].astype(BF16), tb=512)
        x = _peer_dense(peer_u[l].astype(BF16), peer_v[l].T.astype(BF16), xnt, x1, cnt, e1, rank, e2,
                        g_final.reshape(1, -1), tb=512, eb=SUBLANES * N_KEYS)

        for name, tail, i, w in (("hp", tail_p, 0, 1), ("c4p", tail_p, 1, LRU_CONV - 1),
                                 ("c3p", tail_p, 2, SC_CONV - 1), ("hs", tail_s, 0, 1),
                                 ("c4s", tail_s, 1, LRU_CONV - 1), ("c3s", tail_s, 2, SC_CONV - 1)):
            outs[name].append(tail[:, i, SUBLANES - w:, :])

    y_prompt = x[:n_prompt].reshape(x_prompt.shape)
    y_sample = x[n_prompt:].reshape(x_sample.shape)
    hp = jnp.stack(outs["hp"])[:, :, 0, :]
    hs = jnp.stack(outs["hs"])[:, :, 0, :]
    return (y_prompt, y_sample, hp, jnp.stack(outs["c4p"]), jnp.stack(outs["c3p"]),
            hs, jnp.stack(outs["c4s"]), jnp.stack(outs["c3s"]))
```

```python
import functools

import jax
import jax.numpy as jnp
from jax import lax
from jax.experimental import pallas as pl
from jax.experimental.pallas import tpu as pltpu

D_MODEL = 2048
D_LRU = 1024
D_SC = 1024
D_IN = 2 * D_LRU + 3 * D_SC
LRU_HEADS = 16
LRU_HEAD_DIM = D_LRU // LRU_HEADS
LRU_CONV = 4
SC_CONV = 3
LRU_C = 8.0
N_KEYS = 128
N_EXPERTS = N_KEYS * N_KEYS
PEER_HEADS = 8
PEER_TOPK = 16
D_HALF = 128
EPS = 1e-6

LANES = 128
SUBLANES = 8
VMEM_LIMIT = 60 * 1024 * 1024

F32 = jnp.float32
BF16 = jnp.bfloat16
NEG_INF = float("-inf")


def _rms(x, g):
    ms = jnp.mean(x * x, axis=-1, keepdims=True)
    return x * lax.rsqrt(ms + EPS) * g


EXPM1_SERIES_TERMS = 9
EXPM1_SERIES_BOUND = 0.35


def _neg_expm1(x, exp_x):
    s = 1.0 + x * (1.0 / EXPM1_SERIES_TERMS)
    for n in range(EXPM1_SERIES_TERMS - 1, 1, -1):
        s = 1.0 + (x * (1.0 / n)) * s
    return jnp.where(x > -EXPM1_SERIES_BOUND, -x * s, 1.0 - exp_x)


def _nt_dot(a, b):
    return lax.dot_general(a, b, (((1,), (1,)), ((), ())), preferred_element_type=F32)


def _two_group_specs(n_a, n_b, tm, width):
    return [pl.BlockSpec((tm, width), lambda i, *_: (jnp.minimum(i, n_a - 1), 0)),
            pl.BlockSpec((tm, width), lambda i, *_: (jnp.clip(i - n_a, 0, n_b - 1), 0))]


def _norm_matmul_kernel(xa_ref, xb_ref, g_ref, w_ref, o_ref, xn_ref, *, n_a):
    first = pl.program_id(1) == 0
    in_a = pl.program_id(0) < n_a

    @pl.when(jnp.logical_and(first, in_a))
    def _():
        xn_ref[...] = _rms(xa_ref[...], g_ref[...]).astype(BF16)

    @pl.when(jnp.logical_and(first, jnp.logical_not(in_a)))
    def _():
        xn_ref[...] = _rms(xb_ref[...], g_ref[...]).astype(BF16)

    o_ref[...] = jnp.dot(xn_ref[...], w_ref[...], preferred_element_type=F32)


def _norm_matmul(xa, xb, g, w, tm, tn):
    d = xa.shape[1]
    n_a, n_b = xa.shape[0] // tm, xb.shape[0] // tm
    t = xa.shape[0] + xb.shape[0]
    n = w.shape[1]
    return pl.pallas_call(
        functools.partial(_norm_matmul_kernel, n_a=n_a),
        grid=(t // tm, n // tn),
        in_specs=_two_group_specs(n_a, n_b, tm, d) + [
            pl.BlockSpec((1, d), lambda i, j: (0, 0)),
            pl.BlockSpec((d, tn), lambda i, j: (0, j)),
        ],
        out_specs=pl.BlockSpec((tm, tn), lambda i, j: (i, j)),
        out_shape=jax.ShapeDtypeStruct((t, n), F32),
        scratch_shapes=[pltpu.VMEM((tm, d), BF16)],
        compiler_params=pltpu.CompilerParams(
            dimension_semantics=("arbitrary", "arbitrary"), vmem_limit_bytes=VMEM_LIMIT),
        name="norm_in_proj",
    )(xa, xb, g, w)


def _mixer_kernel(z_ref, h0_ref, c4_ref, c3_ref, cw_ref, cb_ref, wg_ref, ba_ref, bx_ref, lam_ref,
                  sw_ref, gl_ref, gs_ref, o_ref, tail_ref,
                  xbuf, ubuf, a_s, b_s, h_s, hcar, *, tt, reset_first):
    t = pl.program_id(1)

    @pl.when(t == 0)
    def _():
        xbuf[0:SUBLANES, :] = c4_ref[0]
        ubuf[0:SUBLANES, :] = c3_ref[0]
        hcar[...] = jnp.broadcast_to(h0_ref[0], (SUBLANES, D_LRU))

    @pl.when(t > 0)
    def _():
        xbuf[0:SUBLANES, :] = xbuf[tt:tt + SUBLANES, :]
        ubuf[0:SUBLANES, :] = ubuf[tt:tt + SUBLANES, :]

    xbuf[SUBLANES:SUBLANES + tt, :] = z_ref[:, 0:D_LRU]
    ubuf[SUBLANES:SUBLANES + tt, :] = (z_ref[:, 2 * D_LRU + D_SC:2 * D_LRU + 2 * D_SC]
                                       * z_ref[:, 2 * D_LRU + 2 * D_SC:D_IN])

    nlam = -lam_ref[...]
    sp = jnp.maximum(nlam, 0.0) + jnp.log1p(jnp.exp(-jnp.abs(nlam)))

    row = lax.broadcasted_iota(jnp.int32, (tt, LANES), 0)
    for j in range(D_LRU // LANES):
        sl = slice(LANES * j, LANES * (j + 1))
        base = SUBLANES - (LRU_CONV - 1)
        xc = xbuf[base:base + tt, sl] * cw_ref[0:1, sl]
        for k in range(1, LRU_CONV):
            xc = xc + xbuf[base + k:base + k + tt, sl] * cw_ref[k:k + 1, sl]
        xc = xc + cb_ref[:, sl]
        gates = jnp.dot(xc.astype(BF16), wg_ref[j], preferred_element_type=F32)
        r = jax.nn.sigmoid(gates[:, 0:LANES] + ba_ref[:, sl])
        i = jax.nn.sigmoid(gates[:, LANES:2 * LANES] + bx_ref[:, sl])
        log_a = (-LRU_C * r) * sp[:, sl]
        a = jnp.exp(log_a)
        mult = jnp.sqrt(_neg_expm1(2.0 * log_a, a * a))
        if reset_first:
            mult = jnp.where(jnp.logical_and(row == 0, t == 0), 1.0, mult)
        a_s[:, sl] = a
        b_s[:, sl] = mult * (i * xc)

    srow = lax.broadcasted_iota(jnp.int32, (SUBLANES, D_LRU), 0)

    def group(gi, carry):
        r0 = pl.multiple_of(gi * SUBLANES, SUBLANES)
        av = a_s[pl.ds(r0, SUBLANES), :]
        bv = b_s[pl.ds(r0, SUBLANES), :]
        for d in (1, 2, 4):
            keep = srow >= d
            a_sh = jnp.where(keep, pltpu.roll(av, d, 0), 1.0)
            b_sh = jnp.where(keep, pltpu.roll(bv, d, 0), 0.0)
            bv = av * b_sh + bv
            av = av * a_sh
        hprev = jnp.broadcast_to(hcar[SUBLANES - 1:SUBLANES, :], (SUBLANES, D_LRU))
        hv = bv + av * hprev
        h_s[pl.ds(r0, SUBLANES), :] = hv
        hcar[...] = hv
        return carry

    lax.fori_loop(0, tt // SUBLANES, group, 0)

    rc = min(tt, 32)
    for c in range(tt // rc):
        rs = slice(rc * c, rc * (c + 1))
        o_lru = h_s[rs, :] * jax.nn.gelu(z_ref[rs, D_LRU:2 * D_LRU])
        o_ref[rs, 0:D_LRU] = _rms(o_lru, gl_ref[...]).astype(BF16)
        base = SUBLANES - (SC_CONV - 1) + rc * c
        uc = ubuf[base:base + rc, :] * sw_ref[0:1, :]
        for k in range(1, SC_CONV):
            uc = uc + ubuf[base + k:base + k + rc, :] * sw_ref[k:k + 1, :]
        o_sc = z_ref[rs, 2 * D_LRU:2 * D_LRU + D_SC] * uc
        o_ref[rs, D_LRU:D_LRU + D_SC] = _rms(o_sc, gs_ref[...]).astype(BF16)

    tail_ref[0, 0] = h_s[tt - SUBLANES:tt, :]
    tail_ref[0, 1] = xbuf[tt:tt + SUBLANES, :]
    tail_ref[0, 2] = ubuf[tt:tt + SUBLANES, :]


def _mixer(z, row0, nseq, tseq, tt, h0, c4, c3, params, reset_first):
    nt = tseq // tt
    blk0 = row0 // tt
    cw, cb, wg, ba, bx, lam, sw, gl, gs = params
    full = lambda a: pl.BlockSpec(a.shape, lambda s, t: (0,) * a.ndim)
    return pl.pallas_call(
        functools.partial(_mixer_kernel, tt=tt, reset_first=reset_first),
        grid=(nseq, nt),
        in_specs=[
            pl.BlockSpec((tt, D_IN), lambda s, t: (blk0 + s * nt + t, 0)),
            pl.BlockSpec((1, 1, D_LRU), lambda s, t: (s, 0, 0)),
            pl.BlockSpec((1, SUBLANES, D_LRU), lambda s, t: (s, 0, 0)),
            pl.BlockSpec((1, SUBLANES, D_SC), lambda s, t: (s, 0, 0)),
            full(cw), full(cb), full(wg), full(ba), full(bx), full(lam), full(sw), full(gl), full(gs),
        ],
        out_specs=[
            pl.BlockSpec((tt, D_LRU + D_SC), lambda s, t: (s * nt + t, 0)),
            pl.BlockSpec((1, 3, SUBLANES, D_LRU), lambda s, t: (s, 0, 0, 0)),
        ],
        out_shape=[
            jax.ShapeDtypeStruct((nseq * tseq, D_LRU + D_SC), BF16),
            jax.ShapeDtypeStruct((nseq, 3, SUBLANES, D_LRU), F32),
        ],
        scratch_shapes=[
            pltpu.VMEM((tt + 2 * SUBLANES, D_LRU), F32),
            pltpu.VMEM((tt + 2 * SUBLANES, D_SC), F32),
            pltpu.VMEM((tt, D_LRU), F32),
            pltpu.VMEM((tt, D_LRU), F32),
            pltpu.VMEM((tt, D_LRU), F32),
            pltpu.VMEM((SUBLANES, D_LRU), F32),
        ],
        compiler_params=pltpu.CompilerParams(
            dimension_semantics=("arbitrary", "arbitrary"), vmem_limit_bytes=VMEM_LIMIT),
        name="mixer_reset" if reset_first else "mixer_cont",
    )(z, h0, c4, c3, cw, cb, wg, ba, bx, lam, sw, gl, gs)


def _out_q_kernel(oa_ref, ob_ref, xa_ref, xb_ref, wout_ref, g_ref, wq_ref, x1_ref, xnt_ref, q_ref,
                  *, n_a):
    in_a = pl.program_id(0) < n_a

    @pl.when(in_a)
    def _():
        x1_ref[...] = xa_ref[...] + jnp.dot(oa_ref[...], wout_ref[...], preferred_element_type=F32)

    @pl.when(jnp.logical_not(in_a))
    def _():
        x1_ref[...] = xb_ref[...] + jnp.dot(ob_ref[...], wout_ref[...], preferred_element_type=F32)

    x1 = x1_ref[...]
    xn32 = _rms(x1, g_ref[...])
    xn = xn32.astype(BF16)
    xnt_ref[...] = xn32.T.astype(BF16)
    q = jnp.dot(xn, wq_ref[...], preferred_element_type=F32).astype(BF16)
    for c in range(2 * PEER_HEADS):
        q_ref[c] = q[:, D_HALF * c:D_HALF * (c + 1)]


def _out_q(oa, ob, xa, xb, w_out, g, w_q, tm):
    n_a, n_b = xa.shape[0] // tm, xb.shape[0] // tm
    t = xa.shape[0] + xb.shape[0]
    const = lambda a: pl.BlockSpec(a.shape, lambda i: (0,) * a.ndim, pipeline_mode=pl.Buffered(1))
    return pl.pallas_call(
        functools.partial(_out_q_kernel, n_a=n_a),
        grid=(t // tm,),
        in_specs=_two_group_specs(n_a, n_b, tm, D_MODEL) + _two_group_specs(n_a, n_b, tm, D_MODEL) + [
            const(w_out), const(g), const(w_q),
        ],
        out_specs=[
            pl.BlockSpec((tm, D_MODEL), lambda i: (i, 0)),
            pl.BlockSpec((D_MODEL, tm), lambda i: (0, i)),
            pl.BlockSpec((2 * PEER_HEADS, tm, D_HALF), lambda i: (0, i, 0)),
        ],
        out_shape=[
            jax.ShapeDtypeStruct((t, D_MODEL), F32),
            jax.ShapeDtypeStruct((D_MODEL, t), BF16),
            jax.ShapeDtypeStruct((2 * PEER_HEADS, t, D_HALF), BF16),
        ],
        compiler_params=pltpu.CompilerParams(
            dimension_semantics=("arbitrary",), vmem_limit_bytes=VMEM_LIMIT),
        name="out_proj_q_proj",
    )(oa, ob, xa, xb, w_out, g, w_q)


N_RANK = PEER_TOPK + 1
BF16_ROWS = 2 * SUBLANES
RANK_SCALE = 2.0 ** 20
KEY_GROUPS = N_KEYS // BF16_ROWS


def _top_values(s, n):
    vals = [jnp.max(s, axis=0, keepdims=True)]
    for _ in range(n - 1):
        vals.append(jnp.max(jnp.where(s < vals[-1], s, NEG_INF), axis=0, keepdims=True))
    return vals


def _peer_select_kernel(q1_ref, q2_ref, k1_ref, k2_ref, cnt_ref, e1_ref, rank_ref, e2_ref, *, tb):
    s1_all = _nt_dot(k1_ref[...], q1_ref[0])
    s2_all = _nt_dot(k2_ref[...], q2_ref[0])
    for l in range(tb // LANES):
        sl = slice(LANES * l, LANES * (l + 1))
        s1 = s1_all[:, sl]
        s2 = s2_all[:, sl]
        v1 = _top_values(s1, N_RANK)
        v2 = _top_values(s2, N_RANK)
        pad = [jnp.full((1, LANES), NEG_INF, F32)] * (3 * SUBLANES - N_RANK)
        v2_col = jnp.concatenate(v2 + pad, axis=0)
        cands = [v1[0] + v2_col] + [v1[a] + v2_col[0:SUBLANES] for a in range(1, N_RANK)]
        cand = jnp.concatenate(cands, axis=0)
        c = _top_values(cand, N_RANK)
        theta = 0.5 * (c[PEER_TOPK - 1] + c[PEER_TOPK])
        z = jnp.sum(jnp.where(cand >= theta, jnp.exp(cand - c[0]), 0.0), axis=0, keepdims=True)
        thr = theta - s1
        cnt = jnp.zeros_like(s1)
        for b in range(PEER_TOPK):
            cnt = cnt + jnp.where(v2[b] >= thr, RANK_SCALE, 0.0)
        rank = jnp.zeros_like(s2)
        for b in range(N_RANK):
            rank = rank + jnp.where(v2[b] > s2, RANK_SCALE, 0.0)
        cnt_ref[0, :, sl] = cnt
        e1_ref[0, :, sl] = jnp.exp(s1 - v1[0]) / z
        rank_ref[0, l] = rank.astype(BF16).reshape(KEY_GROUPS, BF16_ROWS, LANES)
        e2_ref[0, l] = jnp.exp(s2 - v2[0]).astype(BF16).reshape(KEY_GROUPS, BF16_ROWS, LANES)


def _peer_select(q, k1, k2, tb):
    t = q.shape[1]
    rows = jax.ShapeDtypeStruct((PEER_HEADS, N_KEYS, t), F32)
    rows_spec = pl.BlockSpec((1, N_KEYS, tb), lambda i, h: (h, 0, i))
    tiles = jax.ShapeDtypeStruct((PEER_HEADS, t // LANES, KEY_GROUPS, BF16_ROWS, LANES), BF16)
    tiles_spec = pl.BlockSpec((1, tb // LANES, KEY_GROUPS, BF16_ROWS, LANES), lambda i, h: (h, i, 0, 0, 0))
    return pl.pallas_call(
        functools.partial(_peer_select_kernel, tb=tb),
        grid=(t // tb, PEER_HEADS),
        in_specs=[
            pl.BlockSpec((1, tb, D_HALF), lambda i, h: (2 * h, i, 0)),
            pl.BlockSpec((1, tb, D_HALF), lambda i, h: (2 * h + 1, i, 0)),
            pl.BlockSpec((N_KEYS, D_HALF), lambda i, h: (0, 0)),
            pl.BlockSpec((N_KEYS, D_HALF), lambda i, h: (0, 0)),
        ],
        out_specs=[rows_spec, rows_spec, tiles_spec, tiles_spec],
        out_shape=[rows, rows, tiles, tiles],
        compiler_params=pltpu.CompilerParams(
            dimension_semantics=("arbitrary", "arbitrary"), vmem_limit_bytes=VMEM_LIMIT),
        name="peer_select",
    )(q, q, k1, k2)


def _peer_dense_kernel(u_ref, vt_ref, xnt_ref, x1_ref, cnt_ref, e1_ref, rank_ref, e2_ref, g_ref,
                       ya_ref, yb_ref, act0_ref, act1_ref, wt0_ref, wt1_ref, acc_ref, *, tb, eb, ne, n_a):
    s = pl.program_id(0)
    nl = tb // LANES

    @pl.when(s == 0)
    def _():
        act1_ref[...] = jnp.zeros_like(act1_ref)
        wt0_ref[...] = jnp.zeros_like(wt0_ref)
        acc_ref[...] = jnp.zeros_like(acc_ref)

    assert eb // N_KEYS == SUBLANES
    assert nl % 2 == 0 and D_MODEL // N_KEYS == 2 * SUBLANES
    pack = lambda row: jnp.broadcast_to(row, (BF16_ROWS, LANES)).astype(BF16)[None]

    def stages(p):
        act_w, act_r = (act0_ref, act1_ref) if p == 0 else (act1_ref, act0_ref)
        wt_r, wt_w = (wt0_ref, wt1_ref) if p == 0 else (wt1_ref, wt0_ref)
        piece = 0
        for half in range(nl // 2):
            cols = slice(2 * LANES * half, 2 * LANES * (half + 1))
            wt_cols = jnp.concatenate([wt_r[2 * half], wt_r[2 * half + 1]], axis=1)
            for l in (2 * half, 2 * half + 1):
                sl = slice(LANES * l, LANES * (l + 1))
                for ib in range(SUBLANES):
                    rows = slice(N_KEYS * ib, N_KEYS * (ib + 1))
                    j = piece % (2 * SUBLANES)
                    orows = slice(N_KEYS * j, N_KEYS * (j + 1))
                    acc_ref[orows, cols] += jnp.dot(vt_ref[0, orows, :], wt_cols, preferred_element_type=F32)
                    if piece % 2 == 0:
                        a = (piece % (2 * SUBLANES)) // 2
                        arows = slice(N_KEYS * a, N_KEYS * (a + 1))
                        strip = jnp.dot(u_ref[arows, :], xnt_ref[:, cols], preferred_element_type=F32)
                        act_w[2 * half, arows, :] = strip[:, 0:LANES]
                        act_w[2 * half + 1, arows, :] = strip[:, LANES:2 * LANES]
                    w = jnp.zeros((KEY_GROUPS, BF16_ROWS, LANES), BF16)
                    for h in range(PEER_HEADS):
                        cnt = pack(cnt_ref[h, ib:ib + 1, sl])
                        e1 = pack(e1_ref[h, ib:ib + 1, sl])
                        w = w + jnp.maximum(jnp.minimum(e1 * e2_ref[h, l], cnt - rank_ref[h, l]), 0.0)
                    g = jax.nn.gelu(act_r[l, rows, :]).astype(BF16).reshape(KEY_GROUPS, BF16_ROWS, LANES)
                    wt_w[l, rows, :] = (w * g).reshape(N_KEYS, LANES)
                    piece += 1

    for parity in range(2):
        pl.when(s % 2 == parity)(functools.partial(stages, parity))

    @pl.when(jnp.logical_and((s - 1) % ne == 0, s > 1))
    def _():
        x2 = x1_ref[...] + acc_ref[...].T
        acc_ref[...] = jnp.zeros_like(acc_ref)
        y = _rms(x2, g_ref[...])
        done_tile = (s - 2) // ne

        @pl.when(done_tile < n_a)
        def _():
            ya_ref[...] = y

        @pl.when(done_tile >= n_a)
        def _():
            yb_ref[...] = y


def _peer_dense(u, vt, xnt, x1, cnt, e1, rank, e2, g, tb, eb, n_a):
    t = x1.shape[0]
    nt = t // tb
    n_b = nt - n_a
    ne = N_EXPERTS // eb
    tile = lambda s, lag: jnp.clip((s - lag) // ne, 0, nt - 1)
    rows_spec = pl.BlockSpec((PEER_HEADS, SUBLANES, tb), lambda s: (0, (s + ne - 1) % ne, tile(s, 1)))
    tiles_spec = pl.BlockSpec((PEER_HEADS, tb // LANES, KEY_GROUPS, BF16_ROWS, LANES),
                              lambda s: (0, tile(s, 1), 0, 0, 0))
    act_buf = pltpu.VMEM((tb // LANES, eb, LANES), F32)
    wt_buf = pltpu.VMEM((tb // LANES, eb, LANES), BF16)
    return pl.pallas_call(
        functools.partial(_peer_dense_kernel, tb=tb, eb=eb, ne=ne, n_a=n_a),
        grid=(nt * ne + 2,),
        in_specs=[
            pl.BlockSpec((eb, D_MODEL), lambda s: (s % ne, 0)),
            pl.BlockSpec((1, D_MODEL, eb), lambda s: ((s + ne - 2) % ne, 0, 0)),
            pl.BlockSpec((D_MODEL, tb), lambda s: (0, tile(s, 0))),
            pl.BlockSpec((tb, D_MODEL), lambda s: (tile(s, 2), 0), pipeline_mode=pl.Buffered(1)),
            rows_spec, rows_spec, tiles_spec, tiles_spec,
            pl.BlockSpec((1, D_MODEL), lambda s: (0, 0)),
        ],
        out_specs=[
            pl.BlockSpec((tb, D_MODEL), lambda s: (jnp.minimum(tile(s, 2), n_a - 1), 0)),
            pl.BlockSpec((tb, D_MODEL), lambda s: (jnp.clip(tile(s, 2) - n_a, 0, n_b - 1), 0)),
        ],
        out_shape=[jax.ShapeDtypeStruct((n_a * tb, D_MODEL), F32),
                   jax.ShapeDtypeStruct((n_b * tb, D_MODEL), F32)],
        scratch_shapes=[act_buf, act_buf, wt_buf, wt_buf, pltpu.VMEM((D_MODEL, tb), F32)],
        compiler_params=pltpu.CompilerParams(
            dimension_semantics=("arbitrary",), vmem_limit_bytes=VMEM_LIMIT),
        name="peer_dense",
    )(u, vt, xnt, x1, cnt, e1, rank, e2, g)


def _gate_weights(w_a, w_x):
    def blockdiag(w):
        w = w.reshape(LRU_HEADS // 2, 2, LRU_HEAD_DIM, LRU_HEAD_DIM)
        zero = jnp.zeros_like(w[:, 0])
        top = jnp.concatenate([w[:, 0], zero], axis=-1)
        bot = jnp.concatenate([zero, w[:, 1]], axis=-1)
        return jnp.concatenate([top, bot], axis=-2)
    return jnp.concatenate([blockdiag(w_a), blockdiag(w_x)], axis=-1).astype(BF16)


def _right_align(cache, rows):
    n, w, c = cache.shape
    return jnp.concatenate([jnp.zeros((n, rows - w, c), cache.dtype), cache], axis=1)


def kernel(x_prompt, x_sample, state_lru, cache_conv_lru, cache_conv_short, g_mix, w_in, conv_lru_w,
           conv_lru_b, lru_w_a, lru_b_a, lru_w_x, lru_b_x, lru_lambda, conv_sc_w, g_out_lru, g_out_sc,
           w_out, g_ffn, peer_w_q, peer_k1, peer_k2, peer_u, peer_v, g_final):
    depth = g_mix.shape[0]
    assert depth == 1, "the final rmsnorm is fused into the (single) layer's PEER kernel"
    nb, seq, _ = x_prompt.shape
    ndec, dseq, _ = x_sample.shape
    n_prompt = nb * seq
    xa = x_prompt.reshape(n_prompt, D_MODEL)
    xb = x_sample.reshape(ndec * dseq, D_MODEL)

    outs = {k: [] for k in ("hp", "c4p", "c3p", "hs", "c4s", "c3s")}
    for l in range(depth):
        row = lambda a: a[l].reshape(1, -1)
        z = _norm_matmul(xa, xb, row(g_mix), w_in[l].astype(BF16), tm=512, tn=1024)

        params = (conv_lru_w[l], row(conv_lru_b), _gate_weights(lru_w_a[l], lru_w_x[l]), row(lru_b_a),
                  row(lru_b_x), row(lru_lambda), conv_sc_w[l], row(g_out_lru), row(g_out_sc))
        o_p, tail_p = _mixer(
            z, 0, nb, seq, 256, jnp.zeros((nb, 1, D_LRU), F32), jnp.zeros((nb, SUBLANES, D_LRU), F32),
            jnp.zeros((nb, SUBLANES, D_SC), F32), params, True)
        o_s, tail_s = _mixer(
            z, n_prompt, ndec, dseq, dseq, state_lru[l].reshape(ndec, 1, D_LRU),
            _right_align(cache_conv_lru[l], SUBLANES), _right_align(cache_conv_short[l], SUBLANES),
            params, False)
        x1, xnt, q = _out_q(o_p, o_s, xa, xb, w_out[l].astype(BF16), row(g_ffn), peer_w_q[l].astype(BF16),
                            tm=256)
        cnt, e1, rank, e2 = _peer_select(q, peer_k1[l].astype(BF16), peer_k2[l].astype(BF16), tb=512)
        eb = SUBLANES * N_KEYS
        vt = peer_v[l].reshape(N_EXPERTS // eb, eb, D_MODEL).transpose(0, 2, 1).astype(BF16)
        xa, xb = _peer_dense(peer_u[l].astype(BF16), vt, xnt, x1, cnt, e1, rank, e2,
                             g_final.reshape(1, -1), tb=512, eb=eb, n_a=n_prompt // 512)

        for name, tail, i, w in (("hp", tail_p, 0, 1), ("c4p", tail_p, 1, LRU_CONV - 1),
                                 ("c3p", tail_p, 2, SC_CONV - 1), ("hs", tail_s, 0, 1),
                                 ("c4s", tail_s, 1, LRU_CONV - 1), ("c3s", tail_s, 2, SC_CONV - 1)):
            outs[name].append(tail[:, i, SUBLANES - w:, :])

    y_prompt = xa.reshape(x_prompt.shape)
    y_sample = xb.reshape(x_sample.shape)
    hp = jnp.stack(outs["hp"])[:, :, 0, :]
    hs = jnp.stack(outs["hs"])[:, :, 0, :]
    return (y_prompt, y_sample, hp, jnp.stack(outs["c4p"]), jnp.stack(outs["c3p"]),
            hs, jnp.stack(outs["c4s"]), jnp.stack(outs["c3s"]))
```

```python
import functools

import jax
import jax.numpy as jnp
from jax import lax
from jax.experimental import pallas as pl
from jax.experimental.pallas import tpu as pltpu

D_MODEL = 2048
D_LRU = 1024
D_SC = 1024
D_IN = 2 * D_LRU + 3 * D_SC
LRU_HEADS = 16
LRU_HEAD_DIM = D_LRU // LRU_HEADS
LRU_CONV = 4
SC_CONV = 3
LRU_C = 8.0
N_KEYS = 128
N_EXPERTS = N_KEYS * N_KEYS
PEER_HEADS = 8
PEER_TOPK = 16
D_HALF = 128
EPS = 1e-6

LANES = 128
SUBLANES = 8
VMEM_LIMIT = 60 * 1024 * 1024

F32 = jnp.float32
BF16 = jnp.bfloat16
NEG_INF = float("-inf")


def _rms(x, g):
    ms = jnp.mean(x * x, axis=-1, keepdims=True)
    return x * lax.rsqrt(ms + EPS) * g


EXPM1_SERIES_TERMS = 9
EXPM1_SERIES_BOUND = 0.35


def _neg_expm1(x, exp_x):
    s = 1.0 + x * (1.0 / EXPM1_SERIES_TERMS)
    for n in range(EXPM1_SERIES_TERMS - 1, 1, -1):
        s = 1.0 + (x * (1.0 / n)) * s
    return jnp.where(x > -EXPM1_SERIES_BOUND, -x * s, 1.0 - exp_x)


def _nt_dot(a, b):
    return lax.dot_general(a, b, (((1,), (1,)), ((), ())), preferred_element_type=F32)


def _two_group_specs(n_a, n_b, tm, width):
    return [pl.BlockSpec((tm, width), lambda i, *_: (jnp.minimum(i, n_a - 1), 0)),
            pl.BlockSpec((tm, width), lambda i, *_: (jnp.clip(i - n_a, 0, n_b - 1), 0))]


def _norm_matmul_kernel(xa_ref, xb_ref, g_ref, w_ref, o_ref, xn_ref, *, n_a):
    first = pl.program_id(1) == 0
    in_a = pl.program_id(0) < n_a

    @pl.when(jnp.logical_and(first, in_a))
    def _():
        xn_ref[...] = _rms(xa_ref[...], g_ref[...]).astype(BF16)

    @pl.when(jnp.logical_and(first, jnp.logical_not(in_a)))
    def _():
        xn_ref[...] = _rms(xb_ref[...], g_ref[...]).astype(BF16)

    o_ref[...] = jnp.dot(xn_ref[...], w_ref[...], preferred_element_type=F32)


def _norm_matmul(xa, xb, g, w, tm, tn):
    d = xa.shape[1]
    n_a, n_b = xa.shape[0] // tm, xb.shape[0] // tm
    t = xa.shape[0] + xb.shape[0]
    n = w.shape[1]
    return pl.pallas_call(
        functools.partial(_norm_matmul_kernel, n_a=n_a),
        grid=(t // tm, n // tn),
        in_specs=_two_group_specs(n_a, n_b, tm, d) + [
            pl.BlockSpec((1, d), lambda i, j: (0, 0)),
            pl.BlockSpec((d, tn), lambda i, j: (0, j)),
        ],
        out_specs=pl.BlockSpec((tm, tn), lambda i, j: (i, j)),
        out_shape=jax.ShapeDtypeStruct((t, n), F32),
        scratch_shapes=[pltpu.VMEM((tm, d), BF16)],
        compiler_params=pltpu.CompilerParams(
            dimension_semantics=("arbitrary", "arbitrary"), vmem_limit_bytes=VMEM_LIMIT),
        name="norm_in_proj",
    )(xa, xb, g, w)


def _mixer_kernel(z_ref, h0_ref, c4_ref, c3_ref, cw_ref, cb_ref, wg_ref, ba_ref, bx_ref, lam_ref,
                  sw_ref, gl_ref, gs_ref, o_ref, tail_ref,
                  xbuf, ubuf, a_s, b_s, h_s, hcar, *, tt, reset_first):
    t = pl.program_id(1)

    @pl.when(t == 0)
    def _():
        xbuf[0:SUBLANES, :] = c4_ref[0]
        ubuf[0:SUBLANES, :] = c3_ref[0]
        hcar[...] = jnp.broadcast_to(h0_ref[0], (SUBLANES, D_LRU))

    @pl.when(t > 0)
    def _():
        xbuf[0:SUBLANES, :] = xbuf[tt:tt + SUBLANES, :]
        ubuf[0:SUBLANES, :] = ubuf[tt:tt + SUBLANES, :]

    xbuf[SUBLANES:SUBLANES + tt, :] = z_ref[:, 0:D_LRU]
    ubuf[SUBLANES:SUBLANES + tt, :] = (z_ref[:, 2 * D_LRU + D_SC:2 * D_LRU + 2 * D_SC]
                                       * z_ref[:, 2 * D_LRU + 2 * D_SC:D_IN])

    nlam = -lam_ref[...]
    sp = jnp.maximum(nlam, 0.0) + jnp.log1p(jnp.exp(-jnp.abs(nlam)))

    row = lax.broadcasted_iota(jnp.int32, (tt, LANES), 0)
    for j in range(D_LRU // LANES):
        sl = slice(LANES * j, LANES * (j + 1))
        base = SUBLANES - (LRU_CONV - 1)
        xc = xbuf[base:base + tt, sl] * cw_ref[0:1, sl]
        for k in range(1, LRU_CONV):
            xc = xc + xbuf[base + k:base + k + tt, sl] * cw_ref[k:k + 1, sl]
        xc = xc + cb_ref[:, sl]
        gates = jnp.dot(xc.astype(BF16), wg_ref[j], preferred_element_type=F32)
        r = jax.nn.sigmoid(gates[:, 0:LANES] + ba_ref[:, sl])
        i = jax.nn.sigmoid(gates[:, LANES:2 * LANES] + bx_ref[:, sl])
        log_a = (-LRU_C * r) * sp[:, sl]
        a = jnp.exp(log_a)
        mult = jnp.sqrt(_neg_expm1(2.0 * log_a, a * a))
        if reset_first:
            mult = jnp.where(jnp.logical_and(row == 0, t == 0), 1.0, mult)
        a_s[:, sl] = a
        b_s[:, sl] = mult * (i * xc)

    srow = lax.broadcasted_iota(jnp.int32, (SUBLANES, D_LRU), 0)

    def group(gi, carry):
        r0 = pl.multiple_of(gi * SUBLANES, SUBLANES)
        av = a_s[pl.ds(r0, SUBLANES), :]
        bv = b_s[pl.ds(r0, SUBLANES), :]
        for d in (1, 2, 4):
            keep = srow >= d
            a_sh = jnp.where(keep, pltpu.roll(av, d, 0), 1.0)
            b_sh = jnp.where(keep, pltpu.roll(bv, d, 0), 0.0)
            bv = av * b_sh + bv
            av = av * a_sh
        hprev = jnp.broadcast_to(hcar[SUBLANES - 1:SUBLANES, :], (SUBLANES, D_LRU))
        hv = bv + av * hprev
        h_s[pl.ds(r0, SUBLANES), :] = hv
        hcar[...] = hv
        return carry

    lax.fori_loop(0, tt // SUBLANES, group, 0)

    rc = min(tt, 32)
    for c in range(tt // rc):
        rs = slice(rc * c, rc * (c + 1))
        o_lru = h_s[rs, :] * jax.nn.gelu(z_ref[rs, D_LRU:2 * D_LRU])
        o_ref[rs, 0:D_LRU] = _rms(o_lru, gl_ref[...]).astype(BF16)
        base = SUBLANES - (SC_CONV - 1) + rc * c
        uc = ubuf[base:base + rc, :] * sw_ref[0:1, :]
        for k in range(1, SC_CONV):
            uc = uc + ubuf[base + k:base + k + rc, :] * sw_ref[k:k + 1, :]
        o_sc = z_ref[rs, 2 * D_LRU:2 * D_LRU + D_SC] * uc
        o_ref[rs, D_LRU:D_LRU + D_SC] = _rms(o_sc, gs_ref[...]).astype(BF16)

    tail_ref[0, 0] = h_s[tt - SUBLANES:tt, :]
    tail_ref[0, 1] = xbuf[tt:tt + SUBLANES, :]
    tail_ref[0, 2] = ubuf[tt:tt + SUBLANES, :]


def _mixer(z, row0, nseq, tseq, tt, h0, c4, c3, params, reset_first):
    nt = tseq // tt
    blk0 = row0 // tt
    cw, cb, wg, ba, bx, lam, sw, gl, gs = params
    full = lambda a: pl.BlockSpec(a.shape, lambda s, t: (0,) * a.ndim)
    return pl.pallas_call(
        functools.partial(_mixer_kernel, tt=tt, reset_first=reset_first),
        grid=(nseq, nt),
        in_specs=[
            pl.BlockSpec((tt, D_IN), lambda s, t: (blk0 + s * nt + t, 0)),
            pl.BlockSpec((1, 1, D_LRU), lambda s, t: (s, 0, 0)),
            pl.BlockSpec((1, SUBLANES, D_LRU), lambda s, t: (s, 0, 0)),
            pl.BlockSpec((1, SUBLANES, D_SC), lambda s, t: (s, 0, 0)),
            full(cw), full(cb), full(wg), full(ba), full(bx), full(lam), full(sw), full(gl), full(gs),
        ],
        out_specs=[
            pl.BlockSpec((tt, D_LRU + D_SC), lambda s, t: (s * nt + t, 0)),
            pl.BlockSpec((1, 3, SUBLANES, D_LRU), lambda s, t: (s, 0, 0, 0)),
        ],
        out_shape=[
            jax.ShapeDtypeStruct((nseq * tseq, D_LRU + D_SC), BF16),
            jax.ShapeDtypeStruct((nseq, 3, SUBLANES, D_LRU), F32),
        ],
        scratch_shapes=[
            pltpu.VMEM((tt + 2 * SUBLANES, D_LRU), F32),
            pltpu.VMEM((tt + 2 * SUBLANES, D_SC), F32),
            pltpu.VMEM((tt, D_LRU), F32),
            pltpu.VMEM((tt, D_LRU), F32),
            pltpu.VMEM((tt, D_LRU), F32),
            pltpu.VMEM((SUBLANES, D_LRU), F32),
        ],
        compiler_params=pltpu.CompilerParams(
            dimension_semantics=("arbitrary", "arbitrary"), vmem_limit_bytes=VMEM_LIMIT),
        name="mixer_reset" if reset_first else "mixer_cont",
    )(z, h0, c4, c3, cw, cb, wg, ba, bx, lam, sw, gl, gs)


def _out_q_kernel(oa_ref, ob_ref, xa_ref, xb_ref, wout_ref, g_ref, wq_ref, x1_ref, xnt_ref, q_ref,
                  *, n_a):
    in_a = pl.program_id(0) < n_a

    @pl.when(in_a)
    def _():
        x1_ref[...] = xa_ref[...] + jnp.dot(oa_ref[...], wout_ref[...], preferred_element_type=F32)

    @pl.when(jnp.logical_not(in_a))
    def _():
        x1_ref[...] = xb_ref[...] + jnp.dot(ob_ref[...], wout_ref[...], preferred_element_type=F32)

    x1 = x1_ref[...]
    xn32 = _rms(x1, g_ref[...])
    xn = xn32.astype(BF16)
    xnt_ref[...] = xn32.T.astype(BF16)
    q = jnp.dot(xn, wq_ref[...], preferred_element_type=F32).astype(BF16)
    for c in range(2 * PEER_HEADS):
        q_ref[c] = q[:, D_HALF * c:D_HALF * (c + 1)]


def _out_q(oa, ob, xa, xb, w_out, g, w_q, tm):
    n_a, n_b = xa.shape[0] // tm, xb.shape[0] // tm
    t = xa.shape[0] + xb.shape[0]
    const = lambda a: pl.BlockSpec(a.shape, lambda i: (0,) * a.ndim, pipeline_mode=pl.Buffered(1))
    return pl.pallas_call(
        functools.partial(_out_q_kernel, n_a=n_a),
        grid=(t // tm,),
        in_specs=_two_group_specs(n_a, n_b, tm, D_MODEL) + _two_group_specs(n_a, n_b, tm, D_MODEL) + [
            const(w_out), const(g), const(w_q),
        ],
        out_specs=[
            pl.BlockSpec((tm, D_MODEL), lambda i: (i, 0)),
            pl.BlockSpec((D_MODEL, tm), lambda i: (0, i)),
            pl.BlockSpec((2 * PEER_HEADS, tm, D_HALF), lambda i: (0, i, 0)),
        ],
        out_shape=[
            jax.ShapeDtypeStruct((t, D_MODEL), F32),
            jax.ShapeDtypeStruct((D_MODEL, t), BF16),
            jax.ShapeDtypeStruct((2 * PEER_HEADS, t, D_HALF), BF16),
        ],
        compiler_params=pltpu.CompilerParams(
            dimension_semantics=("arbitrary",), vmem_limit_bytes=VMEM_LIMIT),
        name="out_proj_q_proj",
    )(oa, ob, xa, xb, w_out, g, w_q)


N_RANK = PEER_TOPK + 1
BF16_ROWS = 2 * SUBLANES
RANK_SCALE = 2.0 ** 20
KEY_GROUPS = N_KEYS // BF16_ROWS


def _top_values(s, n):
    vals = [jnp.max(s, axis=0, keepdims=True)]
    for _ in range(n - 1):
        vals.append(jnp.max(jnp.where(s < vals[-1], s, NEG_INF), axis=0, keepdims=True))
    return vals


def _peer_select_kernel(q1_ref, q2_ref, k1_ref, k2_ref, cnt_ref, e1_ref, rank_ref, e2_ref, *, tb):
    s1_all = _nt_dot(k1_ref[...], q1_ref[0])
    s2_all = _nt_dot(k2_ref[...], q2_ref[0])
    for l in range(tb // LANES):
        sl = slice(LANES * l, LANES * (l + 1))
        s1 = s1_all[:, sl]
        s2 = s2_all[:, sl]
        v1 = _top_values(s1, N_RANK)
        v2 = _top_values(s2, N_RANK)
        pad = [jnp.full((1, LANES), NEG_INF, F32)] * (3 * SUBLANES - N_RANK)
        v2_col = jnp.concatenate(v2 + pad, axis=0)
        cands = [v1[0] + v2_col] + [v1[a] + v2_col[0:SUBLANES] for a in range(1, N_RANK)]
        cand = jnp.concatenate(cands, axis=0)
        c = _top_values(cand, N_RANK)
        theta = 0.5 * (c[PEER_TOPK - 1] + c[PEER_TOPK])
        z = jnp.sum(jnp.where(cand >= theta, jnp.exp(cand - c[0]), 0.0), axis=0, keepdims=True)
        thr = theta - s1
        cnt = jnp.zeros_like(s1)
        for b in range(PEER_TOPK):
            cnt = cnt + jnp.where(v2[b] >= thr, RANK_SCALE, 0.0)
        rank = jnp.zeros_like(s2)
        for b in range(N_RANK):
            rank = rank + jnp.where(v2[b] > s2, RANK_SCALE, 0.0)
        cnt_ref[0, :, sl] = cnt
        e1_ref[0, :, sl] = jnp.exp(s1 - v1[0]) / z
        rank_ref[0, l] = rank.astype(BF16).reshape(KEY_GROUPS, BF16_ROWS, LANES)
        e2_ref[0, l] = jnp.exp(s2 - v2[0]).astype(BF16).reshape(KEY_GROUPS, BF16_ROWS, LANES)


def _peer_select(q, k1, k2, tb):
    t = q.shape[1]
    rows = jax.ShapeDtypeStruct((PEER_HEADS, N_KEYS, t), F32)
    rows_spec = pl.BlockSpec((1, N_KEYS, tb), lambda i, h: (h, 0, i))
    tiles = jax.ShapeDtypeStruct((PEER_HEADS, t // LANES, KEY_GROUPS, BF16_ROWS, LANES), BF16)
    tiles_spec = pl.BlockSpec((1, tb // LANES, KEY_GROUPS, BF16_ROWS, LANES), lambda i, h: (h, i, 0, 0, 0))
    return pl.pallas_call(
        functools.partial(_peer_select_kernel, tb=tb),
        grid=(t // tb, PEER_HEADS),
        in_specs=[
            pl.BlockSpec((1, tb, D_HALF), lambda i, h: (2 * h, i, 0)),
            pl.BlockSpec((1, tb, D_HALF), lambda i, h: (2 * h + 1, i, 0)),
            pl.BlockSpec((N_KEYS, D_HALF), lambda i, h: (0, 0)),
            pl.BlockSpec((N_KEYS, D_HALF), lambda i, h: (0, 0)),
        ],
        out_specs=[rows_spec, rows_spec, tiles_spec, tiles_spec],
        out_shape=[rows, rows, tiles, tiles],
        compiler_params=pltpu.CompilerParams(
            dimension_semantics=("arbitrary", "arbitrary"), vmem_limit_bytes=VMEM_LIMIT),
        name="peer_select",
    )(q, q, k1, k2)


def _peer_dense_kernel(u_ref, vt_ref, xnt_ref, x1_ref, cnt_ref, e1_ref, rank_ref, e2_ref, g_ref,
                       ya_ref, yb_ref, act0_ref, act1_ref, wt0_ref, wt1_ref, acc_ref, *, tb, eb, ne, n_a,
                       strip_rows):
    s = pl.program_id(0)
    nl = tb // LANES

    @pl.when(s == 0)
    def _():
        act1_ref[...] = jnp.zeros_like(act1_ref)
        wt0_ref[...] = jnp.zeros_like(wt0_ref)
        acc_ref[...] = jnp.zeros_like(acc_ref)

    assert eb // N_KEYS == SUBLANES
    assert nl % 2 == 0
    pieces_per_half = 2 * SUBLANES
    c_every = pieces_per_half * strip_rows // D_MODEL
    a_every = pieces_per_half * strip_rows // eb
    assert c_every >= 2 and a_every > c_every // 2
    pack =lambda row: jnp.broadcast_to(row, (BF16_ROWS, LANES)).astype(BF16)[None]

    def stages(p):
        act_w, act_r = (act0_ref, act1_ref) if p == 0 else (act1_ref, act0_ref)
        wt_r, wt_w = (wt0_ref, wt1_ref) if p == 0 else (wt1_ref, wt0_ref)
        piece = 0
        for half in range(nl // 2):
            cols = slice(2 * LANES * half, 2 * LANES * (half + 1))
            wt_cols = jnp.concatenate([wt_r[2 * half], wt_r[2 * half + 1]], axis=1)
            for l in (2 * half, 2 * half + 1):
                sl = slice(LANES * l, LANES * (l + 1))
                for ib in range(SUBLANES):
                    rows = slice(N_KEYS * ib, N_KEYS * (ib + 1))
                    k = piece % pieces_per_half
                    if k % c_every == 0:
                        j = k // c_every
                        orows = slice(strip_rows * j, strip_rows * (j + 1))
                        acc_ref[orows, cols] += jnp.dot(vt_ref[0, orows, :], wt_cols,
                                                        preferred_element_type=F32)
                    if k % a_every == c_every // 2:
                        a = k // a_every
                        arows = slice(strip_rows * a, strip_rows * (a + 1))
                        strip = jnp.dot(u_ref[arows, :], xnt_ref[:, cols], preferred_element_type=F32)
                        act_w[2 * half, arows, :] = strip[:, 0:LANES]
                        act_w[2 * half + 1, arows, :] = strip[:, LANES:2 * LANES]
                    w = jnp.zeros((KEY_GROUPS, BF16_ROWS, LANES), BF16)
                    for h in range(PEER_HEADS):
                        cnt = pack(cnt_ref[h, ib:ib + 1, sl])
                        e1 = pack(e1_ref[h, ib:ib + 1, sl])
                        w = w + jnp.maximum(jnp.minimum(e1 * e2_ref[h, l], cnt - rank_ref[h, l]), 0.0)
                    g = jax.nn.gelu(act_r[l, rows, :]).astype(BF16).reshape(KEY_GROUPS, BF16_ROWS, LANES)
                    wt_w[l, rows, :] = (w * g).reshape(N_KEYS, LANES)
                    piece += 1

    for parity in range(2):
        pl.when(s % 2 == parity)(functools.partial(stages, parity))

    @pl.when(jnp.logical_and((s - 1) % ne == 0, s > 1))
    def _():
        x2 = x1_ref[...] + acc_ref[...].T
        acc_ref[...] = jnp.zeros_like(acc_ref)
        y = _rms(x2, g_ref[...])
        done_tile = (s - 2) // ne

        @pl.when(done_tile < n_a)
        def _():
            ya_ref[...] = y

        @pl.when(done_tile >= n_a)
        def _():
            yb_ref[...] = y


def _peer_dense(u, vt, xnt, x1, cnt, e1, rank, e2, g, tb, eb, n_a, strip_rows):
    t = x1.shape[0]
    nt = t // tb
    n_b = nt - n_a
    ne = N_EXPERTS // eb
    tile = lambda s, lag: jnp.clip((s - lag) // ne, 0, nt - 1)
    rows_spec = pl.BlockSpec((PEER_HEADS, SUBLANES, tb), lambda s: (0, (s + ne - 1) % ne, tile(s, 1)))
    tiles_spec = pl.BlockSpec((PEER_HEADS, tb // LANES, KEY_GROUPS, BF16_ROWS, LANES),
                              lambda s: (0, tile(s, 1), 0, 0, 0))
    act_buf = pltpu.VMEM((tb // LANES, eb, LANES), F32)
    wt_buf = pltpu.VMEM((tb // LANES, eb, LANES), BF16)
    return pl.pallas_call(
        functools.partial(_peer_dense_kernel, tb=tb, eb=eb, ne=ne, n_a=n_a, strip_rows=strip_rows),
        grid=(nt * ne + 2,),
        in_specs=[
            pl.BlockSpec((eb, D_MODEL), lambda s: (s % ne, 0)),
            pl.BlockSpec((1, D_MODEL, eb), lambda s: ((s + ne - 2) % ne, 0, 0)),
            pl.BlockSpec((D_MODEL, tb), lambda s: (0, tile(s, 0))),
            pl.BlockSpec((tb, D_MODEL), lambda s: (tile(s, 2), 0), pipeline_mode=pl.Buffered(1)),
            rows_spec, rows_spec, tiles_spec, tiles_spec,
            pl.BlockSpec((1, D_MODEL), lambda s: (0, 0)),
        ],
        out_specs=[
            pl.BlockSpec((tb, D_MODEL), lambda s: (jnp.minimum(tile(s, 2), n_a - 1), 0)),
            pl.BlockSpec((tb, D_MODEL), lambda s: (jnp.clip(tile(s, 2) - n_a, 0, n_b - 1), 0)),
        ],
        out_shape=[jax.ShapeDtypeStruct((n_a * tb, D_MODEL), F32),
                   jax.ShapeDtypeStruct((n_b * tb, D_MODEL), F32)],
        scratch_shapes=[act_buf, act_buf, wt_buf, wt_buf, pltpu.VMEM((D_MODEL, tb), F32)],
        compiler_params=pltpu.CompilerParams(
            dimension_semantics=("arbitrary",), vmem_limit_bytes=VMEM_LIMIT),
        name="peer_dense",
    )(u, vt, xnt, x1, cnt, e1, rank, e2, g)


def _gate_weights(w_a, w_x):
    def blockdiag(w):
        w = w.reshape(LRU_HEADS // 2, 2, LRU_HEAD_DIM, LRU_HEAD_DIM)
        zero = jnp.zeros_like(w[:, 0])
        top = jnp.concatenate([w[:, 0], zero], axis=-1)
        bot = jnp.concatenate([zero, w[:, 1]], axis=-1)
        return jnp.concatenate([top, bot], axis=-2)
    return jnp.concatenate([blockdiag(w_a), blockdiag(w_x)], axis=-1).astype(BF16)


def _right_align(cache, rows):
    n, w, c = cache.shape
    return jnp.concatenate([jnp.zeros((n, rows - w, c), cache.dtype), cache], axis=1)


def kernel(x_prompt, x_sample, state_lru, cache_conv_lru, cache_conv_short, g_mix, w_in, conv_lru_w,
           conv_lru_b, lru_w_a, lru_b_a, lru_w_x, lru_b_x, lru_lambda, conv_sc_w, g_out_lru, g_out_sc,
           w_out, g_ffn, peer_w_q, peer_k1, peer_k2, peer_u, peer_v, g_final):
    depth = g_mix.shape[0]
    assert depth == 1, "the final rmsnorm is fused into the (single) layer's PEER kernel"
    nb, seq, _ = x_prompt.shape
    ndec, dseq, _ = x_sample.shape
    n_prompt = nb * seq
    xa = x_prompt.reshape(n_prompt, D_MODEL)
    xb = x_sample.reshape(ndec * dseq, D_MODEL)

    outs = {k: [] for k in ("hp", "c4p", "c3p", "hs", "c4s", "c3s")}
    for l in range(depth):
        row = lambda a: a[l].reshape(1, -1)
        z = _norm_matmul(xa, xb, row(g_mix), w_in[l].astype(BF16), tm=512, tn=1024)

        params = (conv_lru_w[l], row(conv_lru_b), _gate_weights(lru_w_a[l], lru_w_x[l]), row(lru_b_a),
                  row(lru_b_x), row(lru_lambda), conv_sc_w[l], row(g_out_lru), row(g_out_sc))
        o_p, tail_p = _mixer(
            z, 0, nb, seq, 256, jnp.zeros((nb, 1, D_LRU), F32), jnp.zeros((nb, SUBLANES, D_LRU), F32),
            jnp.zeros((nb, SUBLANES, D_SC), F32), params, True)
        o_s, tail_s = _mixer(
            z, n_prompt, ndec, dseq, dseq, state_lru[l].reshape(ndec, 1, D_LRU),
            _right_align(cache_conv_lru[l], SUBLANES), _right_align(cache_conv_short[l], SUBLANES),
            params, False)
        x1, xnt, q = _out_q(o_p, o_s, xa, xb, w_out[l].astype(BF16), row(g_ffn), peer_w_q[l].astype(BF16),
                            tm=256)
        cnt, e1, rank, e2 = _peer_select(q, peer_k1[l].astype(BF16), peer_k2[l].astype(BF16), tb=512)
        eb = SUBLANES * N_KEYS
        vt = peer_v[l].reshape(N_EXPERTS // eb, eb, D_MODEL).transpose(0, 2, 1).astype(BF16)
        xa, xb = _peer_dense(peer_u[l].astype(BF16), vt, xnt, x1, cnt, e1, rank, e2,
                             g_final.reshape(1, -1), tb=512, eb=eb, n_a=n_prompt // 512,
                             strip_rows=256)

        for name, tail, i, w in (("hp", tail_p, 0, 1), ("c4p", tail_p, 1, LRU_CONV - 1),
                                 ("c3p", tail_p, 2, SC_CONV - 1), ("hs", tail_s, 0, 1),
                                 ("c4s", tail_s, 1, LRU_CONV - 1), ("c3s", tail_s, 2, SC_CONV - 1)):
            outs[name].append(tail[:, i, SUBLANES - w:, :])

    y_prompt = xa.reshape(x_prompt.shape)
    y_sample = xb.reshape(x_sample.shape)
    hp = jnp.stack(outs["hp"])[:, :, 0, :]
    hs = jnp.stack(outs["hs"])[:, :, 0, :]
    return (y_prompt, y_sample, hp, jnp.stack(outs["c4p"]), jnp.stack(outs["c3p"]),
            hs, jnp.stack(outs["c4s"]), jnp.stack(outs["c3s"]))
```

```python
import functools

import jax
import jax.numpy as jnp
from jax import lax
from jax.experimental import pallas as pl
from jax.experimental.pallas import tpu as pltpu

D_MODEL = 2048
D_LRU = 1024
D_SC = 1024
D_IN = 2 * D_LRU + 3 * D_SC
LRU_HEADS = 16
LRU_HEAD_DIM = D_LRU // LRU_HEADS
LRU_CONV = 4
SC_CONV = 3
LRU_C = 8.0
N_KEYS = 128
N_EXPERTS = N_KEYS * N_KEYS
PEER_HEADS = 8
PEER_TOPK = 16
D_HALF = 128
EPS = 1e-6

LANES = 128
SUBLANES = 8
VMEM_LIMIT = 60 * 1024 * 1024

F32 = jnp.float32
BF16 = jnp.bfloat16
NEG_INF = float("-inf")


def _rms(x, g):
    ms = jnp.mean(x * x, axis=-1, keepdims=True)
    return x * lax.rsqrt(ms + EPS) * g


EXPM1_SERIES_TERMS = 9
EXPM1_SERIES_BOUND = 0.35


def _neg_expm1(x, exp_x):
    s = 1.0 + x * (1.0 / EXPM1_SERIES_TERMS)
    for n in range(EXPM1_SERIES_TERMS - 1, 1, -1):
        s = 1.0 + (x * (1.0 / n)) * s
    return jnp.where(x > -EXPM1_SERIES_BOUND, -x * s, 1.0 - exp_x)


def _nt_dot(a, b):
    return lax.dot_general(a, b, (((1,), (1,)), ((), ())), preferred_element_type=F32)


def _two_group_specs(n_a, n_b, tm, width):
    return [pl.BlockSpec((tm, width), lambda i, *_: (jnp.minimum(i, n_a - 1), 0)),
            pl.BlockSpec((tm, width), lambda i, *_: (jnp.clip(i - n_a, 0, n_b - 1), 0))]


def _norm_matmul_kernel(xa_ref, xb_ref, g_ref, w_ref, o_ref, *, n_a):
    def project(x_ref):
        xn = _rms(x_ref[...], g_ref[...]).astype(BF16)
        o_ref[...] = jnp.dot(xn, w_ref[...], preferred_element_type=F32).astype(o_ref.dtype)

    in_a = pl.program_id(0) < n_a
    pl.when(in_a)(functools.partial(project, xa_ref))
    pl.when(jnp.logical_not(in_a))(functools.partial(project, xb_ref))


def _norm_matmul(xa, xb, g, w, tm):
    d = xa.shape[1]
    n_a, n_b = xa.shape[0] // tm, xb.shape[0] // tm
    t = xa.shape[0] + xb.shape[0]
    n = w.shape[1]
    return pl.pallas_call(
        functools.partial(_norm_matmul_kernel, n_a=n_a),
        grid=(t // tm,),
        in_specs=_two_group_specs(n_a, n_b, tm, d) + [
            pl.BlockSpec((1, d), lambda i: (0, 0)),
            pl.BlockSpec((d, n), lambda i: (0, 0), pipeline_mode=pl.Buffered(1)),
        ],
        out_specs=pl.BlockSpec((tm, n), lambda i: (i, 0)),
        out_shape=jax.ShapeDtypeStruct((t, n), BF16),
        compiler_params=pltpu.CompilerParams(
            dimension_semantics=("arbitrary",), vmem_limit_bytes=VMEM_LIMIT),
        name="norm_in_proj",
    )(xa, xb, g, w)


def _mixer_kernel(z_ref, h0_ref, c4_ref, c3_ref, cw_ref, cb_ref, wg_ref, ba_ref, bx_ref, lam_ref,
                  sw_ref, gl_ref, gs_ref, o_ref, tail_ref,
                  xbuf, ubuf, a_s, b_s, h_s, hcar, *, tt, reset_first):
    t = pl.program_id(1)

    @pl.when(t == 0)
    def _():
        xbuf[0:SUBLANES, :] = c4_ref[0]
        ubuf[0:SUBLANES, :] = c3_ref[0]
        hcar[...] = jnp.broadcast_to(h0_ref[0], (SUBLANES, D_LRU))

    @pl.when(t > 0)
    def _():
        xbuf[0:SUBLANES, :] = xbuf[tt:tt + SUBLANES, :]
        ubuf[0:SUBLANES, :] = ubuf[tt:tt + SUBLANES, :]

    xbuf[SUBLANES:SUBLANES + tt, :] = z_ref[:, 0:D_LRU].astype(F32)
    ubuf[SUBLANES:SUBLANES + tt, :] = (z_ref[:, 2 * D_LRU + D_SC:2 * D_LRU + 2 * D_SC].astype(F32)
                                       * z_ref[:, 2 * D_LRU + 2 * D_SC:D_IN].astype(F32))

    nlam = -lam_ref[...]
    sp = jnp.maximum(nlam, 0.0) + jnp.log1p(jnp.exp(-jnp.abs(nlam)))

    row = lax.broadcasted_iota(jnp.int32, (tt, LANES), 0)
    for j in range(D_LRU // LANES):
        sl = slice(LANES * j, LANES * (j + 1))
        base = SUBLANES - (LRU_CONV - 1)
        xc = xbuf[base:base + tt, sl] * cw_ref[0:1, sl]
        for k in range(1, LRU_CONV):
            xc = xc + xbuf[base + k:base + k + tt, sl] * cw_ref[k:k + 1, sl]
        xc = xc + cb_ref[:, sl]
        gates = jnp.dot(xc.astype(BF16), wg_ref[j], preferred_element_type=F32)
        r = jax.nn.sigmoid(gates[:, 0:LANES] + ba_ref[:, sl])
        i = jax.nn.sigmoid(gates[:, LANES:2 * LANES] + bx_ref[:, sl])
        log_a = (-LRU_C * r) * sp[:, sl]
        a = jnp.exp(log_a)
        mult = jnp.sqrt(_neg_expm1(2.0 * log_a, a * a))
        if reset_first:
            mult = jnp.where(jnp.logical_and(row == 0, t == 0), 1.0, mult)
        a_s[:, sl] = a
        b_s[:, sl] = mult * (i * xc)

    srow = lax.broadcasted_iota(jnp.int32, (SUBLANES, D_LRU), 0)

    def group(gi, carry):
        r0 = pl.multiple_of(gi * SUBLANES, SUBLANES)
        av = a_s[pl.ds(r0, SUBLANES), :]
        bv = b_s[pl.ds(r0, SUBLANES), :]
        for d in (1, 2, 4):
            keep = srow >= d
            a_sh = jnp.where(keep, pltpu.roll(av, d, 0), 1.0)
            b_sh = jnp.where(keep, pltpu.roll(bv, d, 0), 0.0)
            bv = av * b_sh + bv
            av = av * a_sh
        hprev = jnp.broadcast_to(hcar[SUBLANES - 1:SUBLANES, :], (SUBLANES, D_LRU))
        hv = bv + av * hprev
        h_s[pl.ds(r0, SUBLANES), :] = hv
        hcar[...] = hv
        return carry

    lax.fori_loop(0, tt // SUBLANES, group, 0)

    rc = min(tt, 32)
    for c in range(tt // rc):
        rs = slice(rc * c, rc * (c + 1))
        o_lru = h_s[rs, :] * jax.nn.gelu(z_ref[rs, D_LRU:2 * D_LRU].astype(F32))
        o_ref[rs, 0:D_LRU] = _rms(o_lru, gl_ref[...]).astype(BF16)
        base = SUBLANES - (SC_CONV - 1) + rc * c
        uc = ubuf[base:base + rc, :] * sw_ref[0:1, :]
        for k in range(1, SC_CONV):
            uc = uc + ubuf[base + k:base + k + rc, :] * sw_ref[k:k + 1, :]
        o_sc = z_ref[rs, 2 * D_LRU:2 * D_LRU + D_SC].astype(F32) * uc
        o_ref[rs, D_LRU:D_LRU + D_SC] = _rms(o_sc, gs_ref[...]).astype(BF16)

    tail_ref[0, 0] = h_s[tt - SUBLANES:tt, :]
    tail_ref[0, 1] = xbuf[tt:tt + SUBLANES, :]
    tail_ref[0, 2] = ubuf[tt:tt + SUBLANES, :]


def _mixer(z, row0, nseq, tseq, tt, h0, c4, c3, params, reset_first):
    nt = tseq // tt
    blk0 = row0 // tt
    cw, cb, wg, ba, bx, lam, sw, gl, gs = params
    full = lambda a: pl.BlockSpec(a.shape, lambda s, t: (0,) * a.ndim)
    return pl.pallas_call(
        functools.partial(_mixer_kernel, tt=tt, reset_first=reset_first),
        grid=(nseq, nt),
        in_specs=[
            pl.BlockSpec((tt, D_IN), lambda s, t: (blk0 + s * nt + t, 0)),
            pl.BlockSpec((1, 1, D_LRU), lambda s, t: (s, 0, 0)),
            pl.BlockSpec((1, SUBLANES, D_LRU), lambda s, t: (s, 0, 0)),
            pl.BlockSpec((1, SUBLANES, D_SC), lambda s, t: (s, 0, 0)),
            full(cw), full(cb), full(wg), full(ba), full(bx), full(lam), full(sw), full(gl), full(gs),
        ],
        out_specs=[
            pl.BlockSpec((tt, D_LRU + D_SC), lambda s, t: (s * nt + t, 0)),
            pl.BlockSpec((1, 3, SUBLANES, D_LRU), lambda s, t: (s, 0, 0, 0)),
        ],
        out_shape=[
            jax.ShapeDtypeStruct((nseq * tseq, D_LRU + D_SC), BF16),
            jax.ShapeDtypeStruct((nseq, 3, SUBLANES, D_LRU), F32),
        ],
        scratch_shapes=[
            pltpu.VMEM((tt + 2 * SUBLANES, D_LRU), F32),
            pltpu.VMEM((tt + 2 * SUBLANES, D_SC), F32),
            pltpu.VMEM((tt, D_LRU), F32),
            pltpu.VMEM((tt, D_LRU), F32),
            pltpu.VMEM((tt, D_LRU), F32),
            pltpu.VMEM((SUBLANES, D_LRU), F32),
        ],
        compiler_params=pltpu.CompilerParams(
            dimension_semantics=("arbitrary", "arbitrary"), vmem_limit_bytes=VMEM_LIMIT),
        name="mixer_reset" if reset_first else "mixer_cont",
    )(z, h0, c4, c3, cw, cb, wg, ba, bx, lam, sw, gl, gs)


def _out_q_kernel(oa_ref, ob_ref, xa_ref, xb_ref, wout_ref, g_ref, wq_ref, x1_ref, xnt_ref, q_ref,
                  *, n_a):
    in_a = pl.program_id(0) < n_a

    @pl.when(in_a)
    def _():
        x1_ref[...] = xa_ref[...] + jnp.dot(oa_ref[...], wout_ref[...], preferred_element_type=F32)

    @pl.when(jnp.logical_not(in_a))
    def _():
        x1_ref[...] = xb_ref[...] + jnp.dot(ob_ref[...], wout_ref[...], preferred_element_type=F32)

    x1 = x1_ref[...]
    xn32 = _rms(x1, g_ref[...])
    xn = xn32.astype(BF16)
    xnt_ref[...] = xn32.T.astype(BF16)
    q = jnp.dot(xn, wq_ref[...], preferred_element_type=F32).astype(BF16)
    for c in range(2 * PEER_HEADS):
        q_ref[c] = q[:, D_HALF * c:D_HALF * (c + 1)]


def _out_q(oa, ob, xa, xb, w_out, g, w_q, tm):
    n_a, n_b = xa.shape[0] // tm, xb.shape[0] // tm
    t = xa.shape[0] + xb.shape[0]
    const = lambda a: pl.BlockSpec(a.shape, lambda i: (0,) * a.ndim, pipeline_mode=pl.Buffered(1))
    return pl.pallas_call(
        functools.partial(_out_q_kernel, n_a=n_a),
        grid=(t // tm,),
        in_specs=_two_group_specs(n_a, n_b, tm, D_MODEL) + _two_group_specs(n_a, n_b, tm, D_MODEL) + [
            const(w_out), const(g), const(w_q),
        ],
        out_specs=[
            pl.BlockSpec((tm, D_MODEL), lambda i: (i, 0)),
            pl.BlockSpec((D_MODEL, tm), lambda i: (0, i)),
            pl.BlockSpec((2 * PEER_HEADS, tm, D_HALF), lambda i: (0, i, 0)),
        ],
        out_shape=[
            jax.ShapeDtypeStruct((t, D_MODEL), F32),
            jax.ShapeDtypeStruct((D_MODEL, t), BF16),
            jax.ShapeDtypeStruct((2 * PEER_HEADS, t, D_HALF), BF16),
        ],
        compiler_params=pltpu.CompilerParams(
            dimension_semantics=("arbitrary",), vmem_limit_bytes=VMEM_LIMIT),
        name="out_proj_q_proj",
    )(oa, ob, xa, xb, w_out, g, w_q)


N_RANK = PEER_TOPK + 1
BF16_ROWS = 2 * SUBLANES
RANK_SCALE = 2.0 ** 20
KEY_GROUPS = N_KEYS // BF16_ROWS


def _top_values(s, n):
    vals = [jnp.max(s, axis=0, keepdims=True)]
    for _ in range(n - 1):
        vals.append(jnp.max(jnp.where(s < vals[-1], s, NEG_INF), axis=0, keepdims=True))
    return vals


def _peer_select_kernel(q1_ref, q2_ref, k1_ref, k2_ref, cnt_ref, e1_ref, rank_ref, e2_ref, *, tb):
    s1_all = _nt_dot(k1_ref[...], q1_ref[0])
    s2_all = _nt_dot(k2_ref[...], q2_ref[0])
    for l in range(tb // LANES):
        sl = slice(LANES * l, LANES * (l + 1))
        s1 = s1_all[:, sl]
        s2 = s2_all[:, sl]
        v1 = _top_values(s1, N_RANK)
        v2 = _top_values(s2, N_RANK)
        pad = [jnp.full((1, LANES), NEG_INF, F32)] * (3 * SUBLANES - N_RANK)
        v2_col = jnp.concatenate(v2 + pad, axis=0)
        cands = [v1[0] + v2_col] + [v1[a] + v2_col[0:SUBLANES] for a in range(1, N_RANK)]
        cand = jnp.concatenate(cands, axis=0)
        c = _top_values(cand, N_RANK)
        theta = 0.5 * (c[PEER_TOPK - 1] + c[PEER_TOPK])
        z = jnp.sum(jnp.where(cand >= theta, jnp.exp(cand - c[0]), 0.0), axis=0, keepdims=True)
        thr = theta - s1
        cnt = jnp.zeros_like(s1)
        for b in range(PEER_TOPK):
            cnt = cnt + jnp.where(v2[b] >= thr, RANK_SCALE, 0.0)
        rank = jnp.zeros_like(s2)
        for b in range(N_RANK):
            rank = rank + jnp.where(v2[b] > s2, RANK_SCALE, 0.0)
        cnt_ref[0, :, sl] = cnt
        e1_ref[0, :, sl] = jnp.exp(s1 - v1[0]) / z
        rank_ref[0, l] = rank.astype(BF16).reshape(KEY_GROUPS, BF16_ROWS, LANES)
        e2_ref[0, l] = jnp.exp(s2 - v2[0]).astype(BF16).reshape(KEY_GROUPS, BF16_ROWS, LANES)


def _peer_select(q, k1, k2, tb):
    t = q.shape[1]
    rows = jax.ShapeDtypeStruct((PEER_HEADS, N_KEYS, t), F32)
    rows_spec = pl.BlockSpec((1, N_KEYS, tb), lambda i, h: (h, 0, i))
    tiles = jax.ShapeDtypeStruct((PEER_HEADS, t // LANES, KEY_GROUPS, BF16_ROWS, LANES), BF16)
    tiles_spec = pl.BlockSpec((1, tb // LANES, KEY_GROUPS, BF16_ROWS, LANES), lambda i, h: (h, i, 0, 0, 0))
    return pl.pallas_call(
        functools.partial(_peer_select_kernel, tb=tb),
        grid=(t // tb, PEER_HEADS),
        in_specs=[
            pl.BlockSpec((1, tb, D_HALF), lambda i, h: (2 * h, i, 0)),
            pl.BlockSpec((1, tb, D_HALF), lambda i, h: (2 * h + 1, i, 0)),
            pl.BlockSpec((N_KEYS, D_HALF), lambda i, h: (0, 0)),
            pl.BlockSpec((N_KEYS, D_HALF), lambda i, h: (0, 0)),
        ],
        out_specs=[rows_spec, rows_spec, tiles_spec, tiles_spec],
        out_shape=[rows, rows, tiles, tiles],
        compiler_params=pltpu.CompilerParams(
            dimension_semantics=("arbitrary", "arbitrary"), vmem_limit_bytes=VMEM_LIMIT),
        name="peer_select",
    )(q, q, k1, k2)


def _peer_dense_kernel(u_ref, vt_ref, xnt_ref, x1_ref, cnt_ref, e1_ref, rank_ref, e2_ref, g_ref,
                       ya_ref, yb_ref, act0_ref, act1_ref, wt0_ref, wt1_ref, acc_ref, *, tb, eb, ne, n_a,
                       strip_rows):
    s = pl.program_id(0)
    nl = tb // LANES

    @pl.when(s == 0)
    def _():
        act1_ref[...] = jnp.zeros_like(act1_ref)
        wt0_ref[...] = jnp.zeros_like(wt0_ref)
        acc_ref[...] = jnp.zeros_like(acc_ref)

    assert eb // N_KEYS == SUBLANES
    assert nl % 2 == 0
    pieces_per_half = 2 * SUBLANES
    c_every = pieces_per_half * strip_rows // D_MODEL
    a_every = pieces_per_half * strip_rows // eb
    assert c_every >= 2 and a_every > c_every // 2
    pack =lambda row: jnp.broadcast_to(row, (BF16_ROWS, LANES)).astype(BF16)[None]

    def stages(p):
        act_w, act_r = (act0_ref, act1_ref) if p == 0 else (act1_ref, act0_ref)
        wt_r, wt_w = (wt0_ref, wt1_ref) if p == 0 else (wt1_ref, wt0_ref)
        piece = 0
        for half in range(nl // 2):
            cols = slice(2 * LANES * half, 2 * LANES * (half + 1))
            wt_cols = jnp.concatenate([wt_r[2 * half], wt_r[2 * half + 1]], axis=1)
            for l in (2 * half, 2 * half + 1):
                sl = slice(LANES * l, LANES * (l + 1))
                for ib in range(SUBLANES):
                    rows = slice(N_KEYS * ib, N_KEYS * (ib + 1))
                    k = piece % pieces_per_half
                    if k % c_every == 0:
                        j = k // c_every
                        orows = slice(strip_rows * j, strip_rows * (j + 1))
                        acc_ref[orows, cols] += jnp.dot(vt_ref[0, orows, :], wt_cols,
                                                        preferred_element_type=F32)
                    if k % a_every == c_every // 2:
                        a = k // a_every
                        arows = slice(strip_rows * a, strip_rows * (a + 1))
                        strip = jnp.dot(u_ref[arows, :], xnt_ref[:, cols], preferred_element_type=F32)
                        act_w[2 * half, arows, :] = strip[:, 0:LANES]
                        act_w[2 * half + 1, arows, :] = strip[:, LANES:2 * LANES]
                    w = jnp.zeros((KEY_GROUPS, BF16_ROWS, LANES), BF16)
                    for h in range(PEER_HEADS):
                        cnt = pack(cnt_ref[h, ib:ib + 1, sl])
                        e1 = pack(e1_ref[h, ib:ib + 1, sl])
                        w = w + jnp.maximum(jnp.minimum(e1 * e2_ref[h, l], cnt - rank_ref[h, l]), 0.0)
                    g = jax.nn.gelu(act_r[l, rows, :]).astype(BF16).reshape(KEY_GROUPS, BF16_ROWS, LANES)
                    wt_w[l, rows, :] = (w * g).reshape(N_KEYS, LANES)
                    piece += 1

    for parity in range(2):
        pl.when(s % 2 == parity)(functools.partial(stages, parity))

    @pl.when(jnp.logical_and((s - 1) % ne == 0, s > 1))
    def _():
        x2 = x1_ref[...] + acc_ref[...].T
        acc_ref[...] = jnp.zeros_like(acc_ref)
        y = _rms(x2, g_ref[...])
        done_tile = (s - 2) // ne

        @pl.when(done_tile < n_a)
        def _():
            ya_ref[...] = y

        @pl.when(done_tile >= n_a)
        def _():
            yb_ref[...] = y


def _peer_dense(u, vt, xnt, x1, cnt, e1, rank, e2, g, tb, eb, n_a, strip_rows):
    t = x1.shape[0]
    nt = t // tb
    n_b = nt - n_a
    ne = N_EXPERTS // eb
    tile = lambda s, lag: jnp.clip((s - lag) // ne, 0, nt - 1)
    rows_spec = pl.BlockSpec((PEER_HEADS, SUBLANES, tb), lambda s: (0, (s + ne - 1) % ne, tile(s, 1)))
    tiles_spec = pl.BlockSpec((PEER_HEADS, tb // LANES, KEY_GROUPS, BF16_ROWS, LANES),
                              lambda s: (0, tile(s, 1), 0, 0, 0))
    act_buf = pltpu.VMEM((tb // LANES, eb, LANES), F32)
    wt_buf = pltpu.VMEM((tb // LANES, eb, LANES), BF16)
    return pl.pallas_call(
        functools.partial(_peer_dense_kernel, tb=tb, eb=eb, ne=ne, n_a=n_a, strip_rows=strip_rows),
        grid=(nt * ne + 2,),
        in_specs=[
            pl.BlockSpec((eb, D_MODEL), lambda s: (s % ne, 0)),
            pl.BlockSpec((1, D_MODEL, eb), lambda s: ((s + ne - 2) % ne, 0, 0)),
            pl.BlockSpec((D_MODEL, tb), lambda s: (0, tile(s, 0))),
            pl.BlockSpec((tb, D_MODEL), lambda s: (tile(s, 2), 0), pipeline_mode=pl.Buffered(1)),
            rows_spec, rows_spec, tiles_spec, tiles_spec,
            pl.BlockSpec((1, D_MODEL), lambda s: (0, 0)),
        ],
        out_specs=[
            pl.BlockSpec((tb, D_MODEL), lambda s: (jnp.minimum(tile(s, 2), n_a - 1), 0)),
            pl.BlockSpec((tb, D_MODEL), lambda s: (jnp.clip(tile(s, 2) - n_a, 0, n_b - 1), 0)),
        ],
        out_shape=[jax.ShapeDtypeStruct((n_a * tb, D_MODEL), F32),
                   jax.ShapeDtypeStruct((n_b * tb, D_MODEL), F32)],
        scratch_shapes=[act_buf, act_buf, wt_buf, wt_buf, pltpu.VMEM((D_MODEL, tb), F32)],
        compiler_params=pltpu.CompilerParams(
            dimension_semantics=("arbitrary",), vmem_limit_bytes=VMEM_LIMIT),
        name="peer_dense",
    )(u, vt, xnt, x1, cnt, e1, rank, e2, g)


def _gate_weights(w_a, w_x):
    def blockdiag(w):
        w = w.reshape(LRU_HEADS // 2, 2, LRU_HEAD_DIM, LRU_HEAD_DIM)
        zero = jnp.zeros_like(w[:, 0])
        top = jnp.concatenate([w[:, 0], zero], axis=-1)
        bot = jnp.concatenate([zero, w[:, 1]], axis=-1)
        return jnp.concatenate([top, bot], axis=-2)
    return jnp.concatenate([blockdiag(w_a), blockdiag(w_x)], axis=-1).astype(BF16)


def _right_align(cache, rows):
    n, w, c = cache.shape
    return jnp.concatenate([jnp.zeros((n, rows - w, c), cache.dtype), cache], axis=1)


def kernel(x_prompt, x_sample, state_lru, cache_conv_lru, cache_conv_short, g_mix, w_in, conv_lru_w,
           conv_lru_b, lru_w_a, lru_b_a, lru_w_x, lru_b_x, lru_lambda, conv_sc_w, g_out_lru, g_out_sc,
           w_out, g_ffn, peer_w_q, peer_k1, peer_k2, peer_u, peer_v, g_final):
    depth = g_mix.shape[0]
    assert depth == 1, "the final rmsnorm is fused into the (single) layer's PEER kernel"
    nb, seq, _ = x_prompt.shape
    ndec, dseq, _ = x_sample.shape
    n_prompt = nb * seq
    xa = x_prompt.reshape(n_prompt, D_MODEL)
    xb = x_sample.reshape(ndec * dseq, D_MODEL)

    outs = {k: [] for k in ("hp", "c4p", "c3p", "hs", "c4s", "c3s")}
    for l in range(depth):
        row = lambda a: a[l].reshape(1, -1)
        z = _norm_matmul(xa, xb, row(g_mix), w_in[l].astype(BF16), tm=512)

        params = (conv_lru_w[l], row(conv_lru_b), _gate_weights(lru_w_a[l], lru_w_x[l]), row(lru_b_a),
                  row(lru_b_x), row(lru_lambda), conv_sc_w[l], row(g_out_lru), row(g_out_sc))
        o_p, tail_p = _mixer(
            z, 0, nb, seq, 256, jnp.zeros((nb, 1, D_LRU), F32), jnp.zeros((nb, SUBLANES, D_LRU), F32),
            jnp.zeros((nb, SUBLANES, D_SC), F32), params, True)
        o_s, tail_s = _mixer(
            z, n_prompt, ndec, dseq, dseq, state_lru[l].reshape(ndec, 1, D_LRU),
            _right_align(cache_conv_lru[l], SUBLANES), _right_align(cache_conv_short[l], SUBLANES),
            params, False)
        x1, xnt, q = _out_q(o_p, o_s, xa, xb, w_out[l].astype(BF16), row(g_ffn), peer_w_q[l].astype(BF16),
                            tm=256)
        cnt, e1, rank, e2 = _peer_select(q, peer_k1[l].astype(BF16), peer_k2[l].astype(BF16), tb=512)
        eb = SUBLANES * N_KEYS
        vt = peer_v[l].reshape(N_EXPERTS // eb, eb, D_MODEL).transpose(0, 2, 1).astype(BF16)
        xa, xb = _peer_dense(peer_u[l].astype(BF16), vt, xnt, x1, cnt, e1, rank, e2,
                             g_final.reshape(1, -1), tb=512, eb=eb, n_a=n_prompt // 512,
                             strip_rows=512)

        for name, tail, i, w in (("hp", tail_p, 0, 1), ("c4p", tail_p, 1, LRU_CONV - 1),
                                 ("c3p", tail_p, 2, SC_CONV - 1), ("hs", tail_s, 0, 1),
                                 ("c4s", tail_s, 1, LRU_CONV - 1), ("c3s", tail_s, 2, SC_CONV - 1)):
            outs[name].append(tail[:, i, SUBLANES - w:, :])

    y_prompt = xa.reshape(x_prompt.shape)
    y_sample = xb.reshape(x_sample.shape)
    hp = jnp.stack(outs["hp"])[:, :, 0, :]
    hs = jnp.stack(outs["hs"])[:, :, 0, :]
    return (y_prompt, y_sample, hp, jnp.stack(outs["c4p"]), jnp.stack(outs["c3p"]),
            hs, jnp.stack(outs["c4s"]), jnp.stack(outs["c3s"]))
```

```python
import functools

import jax
import jax.numpy as jnp
from jax import lax
from jax.experimental import pallas as pl
from jax.experimental.pallas import tpu as pltpu

D_MODEL = 2048
D_LRU = 1024
D_SC = 1024
D_IN = 2 * D_LRU + 3 * D_SC
LRU_HEADS = 16
LRU_HEAD_DIM = D_LRU // LRU_HEADS
LRU_CONV = 4
SC_CONV = 3
LRU_C = 8.0
N_KEYS = 128
N_EXPERTS = N_KEYS * N_KEYS
PEER_HEADS = 8
PEER_TOPK = 16
D_HALF = 128
EPS = 1e-6

LANES = 128
SUBLANES = 8
VMEM_LIMIT = 60 * 1024 * 1024

F32 = jnp.float32
BF16 = jnp.bfloat16
NEG_INF = float("-inf")


def _rms(x, g):
    ms = jnp.mean(x * x, axis=-1, keepdims=True)
    return x * lax.rsqrt(ms + EPS) * g


EXPM1_SERIES_TERMS = 9
EXPM1_SERIES_BOUND = 0.35


def _neg_expm1(x, exp_x):
    s = 1.0 + x * (1.0 / EXPM1_SERIES_TERMS)
    for n in range(EXPM1_SERIES_TERMS - 1, 1, -1):
        s = 1.0 + (x * (1.0 / n)) * s
    return jnp.where(x > -EXPM1_SERIES_BOUND, -x * s, 1.0 - exp_x)


def _nt_dot(a, b):
    return lax.dot_general(a, b, (((1,), (1,)), ((), ())), preferred_element_type=F32)


def _two_group_specs(n_a, n_b, tm, width):
    return [pl.BlockSpec((tm, width), lambda i, *_: (jnp.minimum(i, n_a - 1), 0)),
            pl.BlockSpec((tm, width), lambda i, *_: (jnp.clip(i - n_a, 0, n_b - 1), 0),
                         pipeline_mode=pl.Buffered(1))]


def _norm_matmul_kernel(xa_ref, xb_ref, g_ref, w_ref, o_ref, *, n_a):
    def project(x_ref):
        xn = _rms(x_ref[...], g_ref[...]).astype(BF16)
        o_ref[...] = jnp.dot(xn, w_ref[...], preferred_element_type=F32).astype(o_ref.dtype)

    in_a = pl.program_id(0) < n_a
    pl.when(in_a)(functools.partial(project, xa_ref))
    pl.when(jnp.logical_not(in_a))(functools.partial(project, xb_ref))


def _norm_matmul(xa, xb, g, w, tm):
    d = xa.shape[1]
    n_a, n_b = xa.shape[0] // tm, xb.shape[0] // tm
    t = xa.shape[0] + xb.shape[0]
    n = w.shape[1]
    return pl.pallas_call(
        functools.partial(_norm_matmul_kernel, n_a=n_a),
        grid=(t // tm,),
        in_specs=_two_group_specs(n_a, n_b, tm, d) + [
            pl.BlockSpec((1, d), lambda i: (0, 0)),
            pl.BlockSpec((d, n), lambda i: (0, 0), pipeline_mode=pl.Buffered(1)),
        ],
        out_specs=pl.BlockSpec((tm, n), lambda i: (i, 0)),
        out_shape=jax.ShapeDtypeStruct((t, n), BF16),
        compiler_params=pltpu.CompilerParams(
            dimension_semantics=("arbitrary",), vmem_limit_bytes=VMEM_LIMIT),
        name="norm_in_proj",
    )(xa, xb, g, w)


def _mixer_kernel(z_ref, h0_ref, c4_ref, c3_ref, cw_ref, cb_ref, wg_ref, ba_ref, bx_ref, lam_ref,
                  sw_ref, gl_ref, gs_ref, o_ref, tail_ref,
                  xbuf, ubuf, a_s, b_s, h_s, hcar, *, tt, reset_first):
    t = pl.program_id(1)

    @pl.when(t == 0)
    def _():
        xbuf[0:SUBLANES, :] = c4_ref[0]
        ubuf[0:SUBLANES, :] = c3_ref[0]
        hcar[...] = jnp.broadcast_to(h0_ref[0], (SUBLANES, D_LRU))

    @pl.when(t > 0)
    def _():
        xbuf[0:SUBLANES, :] = xbuf[tt:tt + SUBLANES, :]
        ubuf[0:SUBLANES, :] = ubuf[tt:tt + SUBLANES, :]

    xbuf[SUBLANES:SUBLANES + tt, :] = z_ref[:, 0:D_LRU].astype(F32)
    ubuf[SUBLANES:SUBLANES + tt, :] = (z_ref[:, 2 * D_LRU + D_SC:2 * D_LRU + 2 * D_SC].astype(F32)
                                       * z_ref[:, 2 * D_LRU + 2 * D_SC:D_IN].astype(F32))

    nlam = -lam_ref[...]
    sp = jnp.maximum(nlam, 0.0) + jnp.log1p(jnp.exp(-jnp.abs(nlam)))

    row = lax.broadcasted_iota(jnp.int32, (tt, LANES), 0)
    for j in range(D_LRU // LANES):
        sl = slice(LANES * j, LANES * (j + 1))
        base = SUBLANES - (LRU_CONV - 1)
        xc = xbuf[base:base + tt, sl] * cw_ref[0:1, sl]
        for k in range(1, LRU_CONV):
            xc = xc + xbuf[base + k:base + k + tt, sl] * cw_ref[k:k + 1, sl]
        xc = xc + cb_ref[:, sl]
        gates = jnp.dot(xc.astype(BF16), wg_ref[j], preferred_element_type=F32)
        r = jax.nn.sigmoid(gates[:, 0:LANES] + ba_ref[:, sl])
        i = jax.nn.sigmoid(gates[:, LANES:2 * LANES] + bx_ref[:, sl])
        log_a = (-LRU_C * r) * sp[:, sl]
        a = jnp.exp(log_a)
        mult = jnp.sqrt(_neg_expm1(2.0 * log_a, a * a))
        if reset_first:
            mult = jnp.where(jnp.logical_and(row == 0, t == 0), 1.0, mult)
        a_s[:, sl] = a
        b_s[:, sl] = mult * (i * xc)

    srow = lax.broadcasted_iota(jnp.int32, (SUBLANES, D_LRU), 0)

    def group(gi, carry):
        r0 = pl.multiple_of(gi * SUBLANES, SUBLANES)
        av = a_s[pl.ds(r0, SUBLANES), :]
        bv = b_s[pl.ds(r0, SUBLANES), :]
        for d in (1, 2, 4):
            keep = srow >= d
            a_sh = jnp.where(keep, pltpu.roll(av, d, 0), 1.0)
            b_sh = jnp.where(keep, pltpu.roll(bv, d, 0), 0.0)
            bv = av * b_sh + bv
            av = av * a_sh
        hprev = jnp.broadcast_to(hcar[SUBLANES - 1:SUBLANES, :], (SUBLANES, D_LRU))
        hv = bv + av * hprev
        h_s[pl.ds(r0, SUBLANES), :] = hv
        hcar[...] = hv
        return carry

    lax.fori_loop(0, tt // SUBLANES, group, 0)

    rc = min(tt, 32)
    for c in range(tt // rc):
        rs = slice(rc * c, rc * (c + 1))
        o_lru = h_s[rs, :] * jax.nn.gelu(z_ref[rs, D_LRU:2 * D_LRU].astype(F32))
        o_ref[rs, 0:D_LRU] = _rms(o_lru, gl_ref[...]).astype(BF16)
        base = SUBLANES - (SC_CONV - 1) + rc * c
        uc = ubuf[base:base + rc, :] * sw_ref[0:1, :]
        for k in range(1, SC_CONV):
            uc = uc + ubuf[base + k:base + k + rc, :] * sw_ref[k:k + 1, :]
        o_sc = z_ref[rs, 2 * D_LRU:2 * D_LRU + D_SC].astype(F32) * uc
        o_ref[rs, D_LRU:D_LRU + D_SC] = _rms(o_sc, gs_ref[...]).astype(BF16)

    tail_ref[0, 0] = h_s[tt - SUBLANES:tt, :]
    tail_ref[0, 1] = xbuf[tt:tt + SUBLANES, :]
    tail_ref[0, 2] = ubuf[tt:tt + SUBLANES, :]


def _mixer(z, row0, nseq, tseq, tt, h0, c4, c3, params, reset_first):
    nt = tseq // tt
    blk0 = row0 // tt
    cw, cb, wg, ba, bx, lam, sw, gl, gs = params
    full = lambda a: pl.BlockSpec(a.shape, lambda s, t: (0,) * a.ndim)
    return pl.pallas_call(
        functools.partial(_mixer_kernel, tt=tt, reset_first=reset_first),
        grid=(nseq, nt),
        in_specs=[
            pl.BlockSpec((tt, D_IN), lambda s, t: (blk0 + s * nt + t, 0)),
            pl.BlockSpec((1, 1, D_LRU), lambda s, t: (s, 0, 0)),
            pl.BlockSpec((1, SUBLANES, D_LRU), lambda s, t: (s, 0, 0)),
            pl.BlockSpec((1, SUBLANES, D_SC), lambda s, t: (s, 0, 0)),
            full(cw), full(cb), full(wg), full(ba), full(bx), full(lam), full(sw), full(gl), full(gs),
        ],
        out_specs=[
            pl.BlockSpec((tt, D_LRU + D_SC), lambda s, t: (s * nt + t, 0)),
            pl.BlockSpec((1, 3, SUBLANES, D_LRU), lambda s, t: (s, 0, 0, 0)),
        ],
        out_shape=[
            jax.ShapeDtypeStruct((nseq * tseq, D_LRU + D_SC), BF16),
            jax.ShapeDtypeStruct((nseq, 3, SUBLANES, D_LRU), F32),
        ],
        scratch_shapes=[
            pltpu.VMEM((tt + 2 * SUBLANES, D_LRU), F32),
            pltpu.VMEM((tt + 2 * SUBLANES, D_SC), F32),
            pltpu.VMEM((tt, D_LRU), F32),
            pltpu.VMEM((tt, D_LRU), F32),
            pltpu.VMEM((tt, D_LRU), F32),
            pltpu.VMEM((SUBLANES, D_LRU), F32),
        ],
        compiler_params=pltpu.CompilerParams(
            dimension_semantics=("arbitrary", "arbitrary"), vmem_limit_bytes=VMEM_LIMIT),
        name="mixer_reset" if reset_first else "mixer_cont",
    )(z, h0, c4, c3, cw, cb, wg, ba, bx, lam, sw, gl, gs)


def _out_q_kernel(oa_ref, ob_ref, xa_ref, xb_ref, wout_ref, g_ref, wq_ref, x1_ref, xnt_ref, q_ref,
                  *, n_a):
    in_a = pl.program_id(0) < n_a

    @pl.when(in_a)
    def _():
        x1_ref[...] = xa_ref[...] + jnp.dot(oa_ref[...], wout_ref[...], preferred_element_type=F32)

    @pl.when(jnp.logical_not(in_a))
    def _():
        x1_ref[...] = xb_ref[...] + jnp.dot(ob_ref[...], wout_ref[...], preferred_element_type=F32)

    x1 = x1_ref[...]
    xn32 = _rms(x1, g_ref[...])
    xn = xn32.astype(BF16)
    xnt_ref[...] = xn32.T.astype(BF16)
    q = jnp.dot(xn, wq_ref[...], preferred_element_type=F32).astype(BF16)
    for c in range(2 * PEER_HEADS):
        q_ref[c] = q[:, D_HALF * c:D_HALF * (c + 1)]


def _out_q(oa, ob, xa, xb, w_out, g, w_q, tm):
    n_a, n_b = xa.shape[0] // tm, xb.shape[0] // tm
    t = xa.shape[0] + xb.shape[0]
    const = lambda a: pl.BlockSpec(a.shape, lambda i: (0,) * a.ndim, pipeline_mode=pl.Buffered(1))
    return pl.pallas_call(
        functools.partial(_out_q_kernel, n_a=n_a),
        grid=(t // tm,),
        in_specs=_two_group_specs(n_a, n_b, tm, D_MODEL) + _two_group_specs(n_a, n_b, tm, D_MODEL) + [
            const(w_out), const(g), const(w_q),
        ],
        out_specs=[
            pl.BlockSpec((tm, D_MODEL), lambda i: (i, 0)),
            pl.BlockSpec((D_MODEL, tm), lambda i: (0, i)),
            pl.BlockSpec((2 * PEER_HEADS, tm, D_HALF), lambda i: (0, i, 0)),
        ],
        out_shape=[
            jax.ShapeDtypeStruct((t, D_MODEL), F32),
            jax.ShapeDtypeStruct((D_MODEL, t), BF16),
            jax.ShapeDtypeStruct((2 * PEER_HEADS, t, D_HALF), BF16),
        ],
        compiler_params=pltpu.CompilerParams(
            dimension_semantics=("arbitrary",), vmem_limit_bytes=VMEM_LIMIT),
        name="out_proj_q_proj",
    )(oa, ob, xa, xb, w_out, g, w_q)


N_RANK = PEER_TOPK + 1
BF16_ROWS = 2 * SUBLANES
RANK_SCALE = 2.0 ** 20
KEY_GROUPS = N_KEYS // BF16_ROWS


def _top_values(s, n, count_scale=None):
    vals = [jnp.max(s, axis=0, keepdims=True)]
    count = None
    for _ in range(n - 1):
        below = s < vals[-1]
        if count_scale is not None:
            hit = jnp.where(below, count_scale, 0.0)
            count = hit if count is None else count + hit
        vals.append(jnp.max(jnp.where(below, s, NEG_INF), axis=0, keepdims=True))
    if count_scale is None:
        return vals
    return vals, count + jnp.where(s < vals[-1], count_scale, 0.0)


def _peer_select_kernel(q1_ref, q2_ref, k1_ref, k2_ref, cnt_ref, e1_ref, rank_ref, e2_ref, *, tb):
    s1_all = _nt_dot(k1_ref[...], q1_ref[0])
    s2_all = _nt_dot(k2_ref[...], q2_ref[0])
    for l in range(tb // LANES):
        sl = slice(LANES * l, LANES * (l + 1))
        s1 = s1_all[:, sl]
        s2 = s2_all[:, sl]
        v1 = _top_values(s1, N_RANK)
        v2, rank = _top_values(s2, N_RANK, RANK_SCALE)
        pad = [jnp.full((1, LANES), NEG_INF, F32)] * (3 * SUBLANES - N_RANK)
        v2_col = jnp.concatenate(v2 + pad, axis=0)
        cands = [v1[0] + v2_col] + [v1[a] + v2_col[0:SUBLANES] for a in range(1, N_RANK)]
        cand = jnp.concatenate(cands, axis=0)
        c = _top_values(cand, N_RANK)
        theta = 0.5 * (c[PEER_TOPK - 1] + c[PEER_TOPK])
        z = jnp.sum(jnp.where(cand >= theta, jnp.exp(cand - c[0]), 0.0), axis=0, keepdims=True)
        thr = theta - s1
        cnt = jnp.zeros_like(s1)
        for b in range(PEER_TOPK):
            cnt = cnt + jnp.where(v2[b] >= thr, RANK_SCALE, 0.0)
        cnt_ref[0, :, sl] = cnt
        e1_ref[0, :, sl] = jnp.exp(s1 - v1[0]) / z
        rank_ref[0, l] = rank.astype(BF16).reshape(KEY_GROUPS, BF16_ROWS, LANES)
        e2_ref[0, l] = jnp.exp(s2 - v2[0]).astype(BF16).reshape(KEY_GROUPS, BF16_ROWS, LANES)


def _peer_select(q, k1, k2, tb):
    t = q.shape[1]
    rows = jax.ShapeDtypeStruct((PEER_HEADS, N_KEYS, t), F32)
    rows_spec = pl.BlockSpec((1, N_KEYS, tb), lambda i, h: (h, 0, i))
    tiles = jax.ShapeDtypeStruct((PEER_HEADS, t // LANES, KEY_GROUPS, BF16_ROWS, LANES), BF16)
    tiles_spec = pl.BlockSpec((1, tb // LANES, KEY_GROUPS, BF16_ROWS, LANES), lambda i, h: (h, i, 0, 0, 0))
    return pl.pallas_call(
        functools.partial(_peer_select_kernel, tb=tb),
        grid=(t // tb, PEER_HEADS),
        in_specs=[
            pl.BlockSpec((1, tb, D_HALF), lambda i, h: (2 * h, i, 0)),
            pl.BlockSpec((1, tb, D_HALF), lambda i, h: (2 * h + 1, i, 0)),
            pl.BlockSpec((N_KEYS, D_HALF), lambda i, h: (0, 0)),
            pl.BlockSpec((N_KEYS, D_HALF), lambda i, h: (0, 0)),
        ],
        out_specs=[rows_spec, rows_spec, tiles_spec, tiles_spec],
        out_shape=[rows, rows, tiles, tiles],
        compiler_params=pltpu.CompilerParams(
            dimension_semantics=("arbitrary", "arbitrary"), vmem_limit_bytes=VMEM_LIMIT),
        name="peer_select",
    )(q, q, k1, k2)


def _peer_dense_kernel(u_ref, vt_ref, xnt_ref, x1_ref, cnt_ref, e1_ref, rank_ref, e2_ref, g_ref,
                       ya_ref, yb_ref, act0_ref, act1_ref, wt0_ref, wt1_ref, acc_ref, *, tb, eb, ne, n_a,
                       strip_rows):
    s = pl.program_id(0)
    nl = tb // LANES

    @pl.when(s == 0)
    def _():
        act1_ref[...] = jnp.zeros_like(act1_ref)
        wt0_ref[...] = jnp.zeros_like(wt0_ref)
        acc_ref[...] = jnp.zeros_like(acc_ref)

    assert eb // N_KEYS == SUBLANES
    assert nl % 2 == 0
    pieces_per_half = 2 * SUBLANES
    c_every = pieces_per_half * strip_rows // D_MODEL
    a_every = pieces_per_half * strip_rows // eb
    assert c_every >= 2 and a_every > c_every // 2
    pack =lambda row: jnp.broadcast_to(row, (BF16_ROWS, LANES)).astype(BF16)[None]

    def stages(p):
        act_w, act_r = (act0_ref, act1_ref) if p == 0 else (act1_ref, act0_ref)
        wt_r, wt_w = (wt0_ref, wt1_ref) if p == 0 else (wt1_ref, wt0_ref)
        piece = 0
        for half in range(nl // 2):
            cols = slice(2 * LANES * half, 2 * LANES * (half + 1))
            wt_cols = jnp.concatenate([wt_r[2 * half], wt_r[2 * half + 1]], axis=1)
            for l in (2 * half, 2 * half + 1):
                sl = slice(LANES * l, LANES * (l + 1))
                for ib in range(SUBLANES):
                    rows = slice(N_KEYS * ib, N_KEYS * (ib + 1))
                    k = piece % pieces_per_half
                    if k % c_every == 0:
                        j = k // c_every
                        orows = slice(strip_rows * j, strip_rows * (j + 1))
                        acc_ref[orows, cols] += jnp.dot(vt_ref[0, orows, :], wt_cols,
                                                        preferred_element_type=F32)
                    if k % a_every == c_every // 2:
                        a = k // a_every
                        arows = slice(strip_rows * a, strip_rows * (a + 1))
                        strip = jnp.dot(u_ref[arows, :], xnt_ref[:, cols], preferred_element_type=F32)
                        act_w[2 * half, arows, :] = strip[:, 0:LANES]
                        act_w[2 * half + 1, arows, :] = strip[:, LANES:2 * LANES]
                    w = jnp.zeros((KEY_GROUPS, BF16_ROWS, LANES), BF16)
                    for h in range(PEER_HEADS):
                        cnt = pack(cnt_ref[h, ib:ib + 1, sl])
                        e1 = pack(e1_ref[h, ib:ib + 1, sl])
                        w = w + jnp.maximum(jnp.minimum(e1 * e2_ref[h, l], cnt - rank_ref[h, l]), 0.0)
                    g = jax.nn.gelu(act_r[l, rows, :]).astype(BF16).reshape(KEY_GROUPS, BF16_ROWS, LANES)
                    wt_w[l, rows, :] = (w * g).reshape(N_KEYS, LANES)
                    piece += 1

    for parity in range(2):
        pl.when(s % 2 == parity)(functools.partial(stages, parity))

    @pl.when(jnp.logical_and((s - 1) % ne == 0, s > 1))
    def _():
        x2 = x1_ref[...] + acc_ref[...].T
        acc_ref[...] = jnp.zeros_like(acc_ref)
        y = _rms(x2, g_ref[...])
        done_tile = (s - 2) // ne

        @pl.when(done_tile < n_a)
        def _():
            ya_ref[...] = y

        @pl.when(done_tile >= n_a)
        def _():
            yb_ref[...] = y


def _peer_dense(u, vt, xnt, x1, cnt, e1, rank, e2, g, tb, eb, n_a, strip_rows):
    t = x1.shape[0]
    nt = t // tb
    n_b = nt - n_a
    ne = N_EXPERTS // eb
    tile = lambda s, lag: jnp.clip((s - lag) // ne, 0, nt - 1)
    rows_spec = pl.BlockSpec((PEER_HEADS, SUBLANES, tb), lambda s: (0, (s + ne - 1) % ne, tile(s, 1)))
    tiles_spec = pl.BlockSpec((PEER_HEADS, tb // LANES, KEY_GROUPS, BF16_ROWS, LANES),
                              lambda s: (0, tile(s, 1), 0, 0, 0))
    act_buf = pltpu.VMEM((tb // LANES, eb, LANES), F32)
    wt_buf = pltpu.VMEM((tb // LANES, eb, LANES), BF16)
    return pl.pallas_call(
        functools.partial(_peer_dense_kernel, tb=tb, eb=eb, ne=ne, n_a=n_a, strip_rows=strip_rows),
        grid=(nt * ne + 2,),
        in_specs=[
            pl.BlockSpec((eb, D_MODEL), lambda s: (s % ne, 0)),
            pl.BlockSpec((1, D_MODEL, eb), lambda s: ((s + ne - 2) % ne, 0, 0)),
            pl.BlockSpec((D_MODEL, tb), lambda s: (0, tile(s, 0))),
            pl.BlockSpec((tb, D_MODEL), lambda s: (tile(s, 2), 0), pipeline_mode=pl.Buffered(1)),
            rows_spec, rows_spec, tiles_spec, tiles_spec,
            pl.BlockSpec((1, D_MODEL), lambda s: (0, 0)),
        ],
        out_specs=[
            pl.BlockSpec((tb, D_MODEL), lambda s: (jnp.minimum(tile(s, 2), n_a - 1), 0)),
            pl.BlockSpec((tb, D_MODEL), lambda s: (jnp.clip(tile(s, 2) - n_a, 0, n_b - 1), 0)),
        ],
        out_shape=[jax.ShapeDtypeStruct((n_a * tb, D_MODEL), F32),
                   jax.ShapeDtypeStruct((n_b * tb, D_MODEL), F32)],
        scratch_shapes=[act_buf, act_buf, wt_buf, wt_buf, pltpu.VMEM((D_MODEL, tb), F32)],
        compiler_params=pltpu.CompilerParams(
            dimension_semantics=("arbitrary",), vmem_limit_bytes=VMEM_LIMIT),
        name="peer_dense",
    )(u, vt, xnt, x1, cnt, e1, rank, e2, g)


def _gate_weights(w_a, w_x):
    def blockdiag(w):
        w = w.reshape(LRU_HEADS // 2, 2, LRU_HEAD_DIM, LRU_HEAD_DIM)
        zero = jnp.zeros_like(w[:, 0])
        top = jnp.concatenate([w[:, 0], zero], axis=-1)
        bot = jnp.concatenate([zero, w[:, 1]], axis=-1)
        return jnp.concatenate([top, bot], axis=-2)
    return jnp.concatenate([blockdiag(w_a), blockdiag(w_x)], axis=-1).astype(BF16)


def _right_align(cache, rows):
    n, w, c = cache.shape
    return jnp.concatenate([jnp.zeros((n, rows - w, c), cache.dtype), cache], axis=1)


def kernel(x_prompt, x_sample, state_lru, cache_conv_lru, cache_conv_short, g_mix, w_in, conv_lru_w,
           conv_lru_b, lru_w_a, lru_b_a, lru_w_x, lru_b_x, lru_lambda, conv_sc_w, g_out_lru, g_out_sc,
           w_out, g_ffn, peer_w_q, peer_k1, peer_k2, peer_u, peer_v, g_final):
    depth = g_mix.shape[0]
    assert depth == 1, "the final rmsnorm is fused into the (single) layer's PEER kernel"
    nb, seq, _ = x_prompt.shape
    ndec, dseq, _ = x_sample.shape
    n_prompt = nb * seq
    xa = x_prompt.reshape(n_prompt, D_MODEL)
    xb = x_sample.reshape(ndec * dseq, D_MODEL)

    outs = {k: [] for k in ("hp", "c4p", "c3p", "hs", "c4s", "c3s")}
    for l in range(depth):
        row = lambda a: a[l].reshape(1, -1)
        z = _norm_matmul(xa, xb, row(g_mix), w_in[l].astype(BF16), tm=512)

        params = (conv_lru_w[l], row(conv_lru_b), _gate_weights(lru_w_a[l], lru_w_x[l]), row(lru_b_a),
                  row(lru_b_x), row(lru_lambda), conv_sc_w[l], row(g_out_lru), row(g_out_sc))
        o_p, tail_p = _mixer(
            z, 0, nb, seq, 256, jnp.zeros((nb, 1, D_LRU), F32), jnp.zeros((nb, SUBLANES, D_LRU), F32),
            jnp.zeros((nb, SUBLANES, D_SC), F32), params, True)
        o_s, tail_s = _mixer(
            z, n_prompt, ndec, dseq, dseq, state_lru[l].reshape(ndec, 1, D_LRU),
            _right_align(cache_conv_lru[l], SUBLANES), _right_align(cache_conv_short[l], SUBLANES),
            params, False)
        x1, xnt, q = _out_q(o_p, o_s, xa, xb, w_out[l].astype(BF16), row(g_ffn), peer_w_q[l].astype(BF16),
                            tm=512)
        cnt, e1, rank, e2 = _peer_select(q, peer_k1[l].astype(BF16), peer_k2[l].astype(BF16), tb=512)
        eb = SUBLANES * N_KEYS
        vt = peer_v[l].reshape(N_EXPERTS // eb, eb, D_MODEL).transpose(0, 2, 1).astype(BF16)
        xa, xb = _peer_dense(peer_u[l].astype(BF16), vt, xnt, x1, cnt, e1, rank, e2,
                             g_final.reshape(1, -1), tb=512, eb=eb, n_a=n_prompt // 512,
                             strip_rows=256)

        for name, tail, i, w in (("hp", tail_p, 0, 1), ("c4p", tail_p, 1, LRU_CONV - 1),
                                 ("c3p", tail_p, 2, SC_CONV - 1), ("hs", tail_s, 0, 1),
                                 ("c4s", tail_s, 1, LRU_CONV - 1), ("c3s", tail_s, 2, SC_CONV - 1)):
            outs[name].append(tail[:, i, SUBLANES - w:, :])

    y_prompt = xa.reshape(x_prompt.shape)
    y_sample = xb.reshape(x_sample.shape)
    hp = jnp.stack(outs["hp"])[:, :, 0, :]
    hs = jnp.stack(outs["hs"])[:, :, 0, :]
    return (y_prompt, y_sample, hp, jnp.stack(outs["c4p"]), jnp.stack(outs["c3p"]),
            hs, jnp.stack(outs["c4s"]), jnp.stack(outs["c3s"]))
```

```python
import functools

import jax
import jax.numpy as jnp
from jax import lax
from jax.experimental import pallas as pl
from jax.experimental.pallas import tpu as pltpu

D_MODEL = 2048
D_LRU = 1024
D_SC = 1024
D_IN = 2 * D_LRU + 3 * D_SC
LRU_HEADS = 16
LRU_HEAD_DIM = D_LRU // LRU_HEADS
LRU_CONV = 4
SC_CONV = 3
LRU_C = 8.0
N_KEYS = 128
N_EXPERTS = N_KEYS * N_KEYS
PEER_HEADS = 8
PEER_TOPK = 16
D_HALF = 128
EPS = 1e-6

LANES = 128
SUBLANES = 8
VMEM_LIMIT = 60 * 1024 * 1024

F32 = jnp.float32
BF16 = jnp.bfloat16
NEG_INF = float("-inf")


def _rms(x, g):
    ms = jnp.mean(x * x, axis=-1, keepdims=True)
    return x * lax.rsqrt(ms + EPS) * g


EXPM1_SERIES_TERMS = 9
EXPM1_SERIES_BOUND = 0.35


def _neg_expm1(x, exp_x):
    s = 1.0 + x * (1.0 / EXPM1_SERIES_TERMS)
    for n in range(EXPM1_SERIES_TERMS - 1, 1, -1):
        s = 1.0 + (x * (1.0 / n)) * s
    return jnp.where(x > -EXPM1_SERIES_BOUND, -x * s, 1.0 - exp_x)


def _nt_dot(a, b):
    return lax.dot_general(a, b, (((1,), (1,)), ((), ())), preferred_element_type=F32)


def _two_group_specs(n_a, n_b, tm, width):
    return [pl.BlockSpec((tm, width), lambda i, *_: (jnp.minimum(i, n_a - 1), 0)),
            pl.BlockSpec((tm, width), lambda i, *_: (jnp.clip(i - n_a, 0, n_b - 1), 0),
                         pipeline_mode=pl.Buffered(1))]


def _norm_matmul_kernel(xa_ref, xb_ref, g_ref, w_ref, o_ref, *, n_a):
    def project(x_ref):
        xn = _rms(x_ref[...], g_ref[...]).astype(BF16)
        o_ref[...] = jnp.dot(xn, w_ref[...], preferred_element_type=F32).astype(o_ref.dtype)

    in_a = pl.program_id(0) < n_a
    pl.when(in_a)(functools.partial(project, xa_ref))
    pl.when(jnp.logical_not(in_a))(functools.partial(project, xb_ref))


def _norm_matmul(xa, xb, g, w, tm):
    d = xa.shape[1]
    n_a, n_b = xa.shape[0] // tm, xb.shape[0] // tm
    t = xa.shape[0] + xb.shape[0]
    n = w.shape[1]
    return pl.pallas_call(
        functools.partial(_norm_matmul_kernel, n_a=n_a),
        grid=(t // tm,),
        in_specs=_two_group_specs(n_a, n_b, tm, d) + [
            pl.BlockSpec((1, d), lambda i: (0, 0)),
            pl.BlockSpec((d, n), lambda i: (0, 0), pipeline_mode=pl.Buffered(1)),
        ],
        out_specs=pl.BlockSpec((tm, n), lambda i: (i, 0)),
        out_shape=jax.ShapeDtypeStruct((t, n), BF16),
        compiler_params=pltpu.CompilerParams(
            dimension_semantics=("arbitrary",), vmem_limit_bytes=VMEM_LIMIT),
        name="norm_in_proj",
    )(xa, xb, g, w)


def _mixer_kernel(z_ref, h0_ref, c4_ref, c3_ref, cw_ref, cb_ref, wg_ref, ba_ref, bx_ref, lam_ref,
                  sw_ref, gl_ref, gs_ref, o_ref, tail_ref,
                  xbuf, ubuf, a_s, b_s, h_s, hcar, *, tt, reset_first):
    t = pl.program_id(1)

    @pl.when(t == 0)
    def _():
        xbuf[0:SUBLANES, :] = c4_ref[0]
        ubuf[0:SUBLANES, :] = c3_ref[0]
        hcar[...] = jnp.broadcast_to(h0_ref[0], (SUBLANES, D_LRU))

    @pl.when(t > 0)
    def _():
        xbuf[0:SUBLANES, :] = xbuf[tt:tt + SUBLANES, :]
        ubuf[0:SUBLANES, :] = ubuf[tt:tt + SUBLANES, :]

    xbuf[SUBLANES:SUBLANES + tt, :] = z_ref[:, 0:D_LRU].astype(F32)
    ubuf[SUBLANES:SUBLANES + tt, :] = (z_ref[:, 2 * D_LRU + D_SC:2 * D_LRU + 2 * D_SC].astype(F32)
                                       * z_ref[:, 2 * D_LRU + 2 * D_SC:D_IN].astype(F32))

    nlam = -lam_ref[...]
    sp = jnp.maximum(nlam, 0.0) + jnp.log1p(jnp.exp(-jnp.abs(nlam)))

    row = lax.broadcasted_iota(jnp.int32, (tt, LANES), 0)
    for j in range(D_LRU // LANES):
        sl = slice(LANES * j, LANES * (j + 1))
        base = SUBLANES - (LRU_CONV - 1)
        xc = xbuf[base:base + tt, sl] * cw_ref[0:1, sl]
        for k in range(1, LRU_CONV):
            xc = xc + xbuf[base + k:base + k + tt, sl] * cw_ref[k:k + 1, sl]
        xc = xc + cb_ref[:, sl]
        gates = jnp.dot(xc.astype(BF16), wg_ref[j], preferred_element_type=F32)
        r = jax.nn.sigmoid(gates[:, 0:LANES] + ba_ref[:, sl])
        i = jax.nn.sigmoid(gates[:, LANES:2 * LANES] + bx_ref[:, sl])
        log_a = (-LRU_C * r) * sp[:, sl]
        a = jnp.exp(log_a)
        mult = jnp.sqrt(_neg_expm1(2.0 * log_a, a * a))
        if reset_first:
            mult = jnp.where(jnp.logical_and(row == 0, t == 0), 1.0, mult)
        a_s[:, sl] = a
        b_s[:, sl] = mult * (i * xc)

    srow = lax.broadcasted_iota(jnp.int32, (SUBLANES, D_LRU), 0)

    def group(gi, carry):
        r0 = pl.multiple_of(gi * SUBLANES, SUBLANES)
        av = a_s[pl.ds(r0, SUBLANES), :]
        bv = b_s[pl.ds(r0, SUBLANES), :]
        for d in (1, 2, 4):
            keep = srow >= d
            a_sh = jnp.where(keep, pltpu.roll(av, d, 0), 1.0)
            b_sh = jnp.where(keep, pltpu.roll(bv, d, 0), 0.0)
            bv = av * b_sh + bv
            av = av * a_sh
        hprev = jnp.broadcast_to(hcar[SUBLANES - 1:SUBLANES, :], (SUBLANES, D_LRU))
        hv = bv + av * hprev
        h_s[pl.ds(r0, SUBLANES), :] = hv
        hcar[...] = hv
        return carry

    lax.fori_loop(0, tt // SUBLANES, group, 0)

    rc = min(tt, 32)
    for c in range(tt // rc):
        rs = slice(rc * c, rc * (c + 1))
        o_lru = h_s[rs, :] * jax.nn.gelu(z_ref[rs, D_LRU:2 * D_LRU].astype(F32))
        o_ref[rs, 0:D_LRU] = _rms(o_lru, gl_ref[...]).astype(BF16)
        base = SUBLANES - (SC_CONV - 1) + rc * c
        uc = ubuf[base:base + rc, :] * sw_ref[0:1, :]
        for k in range(1, SC_CONV):
            uc = uc + ubuf[base + k:base + k + rc, :] * sw_ref[k:k + 1, :]
        o_sc = z_ref[rs, 2 * D_LRU:2 * D_LRU + D_SC].astype(F32) * uc
        o_ref[rs, D_LRU:D_LRU + D_SC] = _rms(o_sc, gs_ref[...]).astype(BF16)

    tail_ref[0, 0] = h_s[tt - SUBLANES:tt, :]
    tail_ref[0, 1] = xbuf[tt:tt + SUBLANES, :]
    tail_ref[0, 2] = ubuf[tt:tt + SUBLANES, :]


def _mixer(z, row0, nseq, tseq, tt, h0, c4, c3, params, reset_first):
    nt = tseq // tt
    blk0 = row0 // tt
    cw, cb, wg, ba, bx, lam, sw, gl, gs = params
    full = lambda a: pl.BlockSpec(a.shape, lambda s, t: (0,) * a.ndim)
    return pl.pallas_call(
        functools.partial(_mixer_kernel, tt=tt, reset_first=reset_first),
        grid=(nseq, nt),
        in_specs=[
            pl.BlockSpec((tt, D_IN), lambda s, t: (blk0 + s * nt + t, 0)),
            pl.BlockSpec((1, 1, D_LRU), lambda s, t: (s, 0, 0)),
            pl.BlockSpec((1, SUBLANES, D_LRU), lambda s, t: (s, 0, 0)),
            pl.BlockSpec((1, SUBLANES, D_SC), lambda s, t: (s, 0, 0)),
            full(cw), full(cb), full(wg), full(ba), full(bx), full(lam), full(sw), full(gl), full(gs),
        ],
        out_specs=[
            pl.BlockSpec((tt, D_LRU + D_SC), lambda s, t: (s * nt + t, 0)),
            pl.BlockSpec((1, 3, SUBLANES, D_LRU), lambda s, t: (s, 0, 0, 0)),
        ],
        out_shape=[
            jax.ShapeDtypeStruct((nseq * tseq, D_LRU + D_SC), BF16),
            jax.ShapeDtypeStruct((nseq, 3, SUBLANES, D_LRU), F32),
        ],
        scratch_shapes=[
            pltpu.VMEM((tt + 2 * SUBLANES, D_LRU), F32),
            pltpu.VMEM((tt + 2 * SUBLANES, D_SC), F32),
            pltpu.VMEM((tt, D_LRU), F32),
            pltpu.VMEM((tt, D_LRU), F32),
            pltpu.VMEM((tt, D_LRU), F32),
            pltpu.VMEM((SUBLANES, D_LRU), F32),
        ],
        compiler_params=pltpu.CompilerParams(
            dimension_semantics=("arbitrary", "arbitrary"), vmem_limit_bytes=VMEM_LIMIT),
        name="mixer_reset" if reset_first else "mixer_cont",
    )(z, h0, c4, c3, cw, cb, wg, ba, bx, lam, sw, gl, gs)


def _out_q_kernel(oa_ref, ob_ref, xa_ref, xb_ref, wout_ref, g_ref, wq_ref, x1_ref, xnt_ref, q_ref,
                  *, n_a):
    in_a = pl.program_id(0) < n_a

    @pl.when(in_a)
    def _():
        x1_ref[...] = xa_ref[...] + jnp.dot(oa_ref[...], wout_ref[...], preferred_element_type=F32)

    @pl.when(jnp.logical_not(in_a))
    def _():
        x1_ref[...] = xb_ref[...] + jnp.dot(ob_ref[...], wout_ref[...], preferred_element_type=F32)

    x1 = x1_ref[...]
    xn32 = _rms(x1, g_ref[...])
    xn = xn32.astype(BF16)
    xnt_ref[...] = xn32.T.astype(BF16)
    q = jnp.dot(xn, wq_ref[...], preferred_element_type=F32).astype(BF16)
    for c in range(2 * PEER_HEADS):
        q_ref[c] = q[:, D_HALF * c:D_HALF * (c + 1)]


def _out_q(oa, ob, xa, xb, w_out, g, w_q, tm):
    n_a, n_b = xa.shape[0] // tm, xb.shape[0] // tm
    t = xa.shape[0] + xb.shape[0]
    const = lambda a: pl.BlockSpec(a.shape, lambda i: (0,) * a.ndim, pipeline_mode=pl.Buffered(1))
    return pl.pallas_call(
        functools.partial(_out_q_kernel, n_a=n_a),
        grid=(t // tm,),
        in_specs=_two_group_specs(n_a, n_b, tm, D_MODEL) + _two_group_specs(n_a, n_b, tm, D_MODEL) + [
            const(w_out), const(g), const(w_q),
        ],
        out_specs=[
            pl.BlockSpec((tm, D_MODEL), lambda i: (i, 0)),
            pl.BlockSpec((D_MODEL, tm), lambda i: (0, i)),
            pl.BlockSpec((2 * PEER_HEADS, tm, D_HALF), lambda i: (0, i, 0)),
        ],
        out_shape=[
            jax.ShapeDtypeStruct((t, D_MODEL), F32),
            jax.ShapeDtypeStruct((D_MODEL, t), BF16),
            jax.ShapeDtypeStruct((2 * PEER_HEADS, t, D_HALF), BF16),
        ],
        compiler_params=pltpu.CompilerParams(
            dimension_semantics=("arbitrary",), vmem_limit_bytes=VMEM_LIMIT),
        name="out_proj_q_proj",
    )(oa, ob, xa, xb, w_out, g, w_q)


N_RANK = PEER_TOPK + 1
BF16_ROWS = 2 * SUBLANES
RANK_SCALE = 2.0 ** 20
KEY_GROUPS = N_KEYS // BF16_ROWS


def _top_values(s, n, count_scale=None):
    vals = [jnp.max(s, axis=0, keepdims=True)]
    count = jnp.zeros_like(s)
    for r in range(1, n):
        below = s < vals[-1]
        if count_scale is not None:
            count = jnp.where(below, r * count_scale, count)
        vals.append(jnp.max(jnp.where(below, s, NEG_INF), axis=0, keepdims=True))
    if count_scale is None:
        return vals
    return vals, jnp.where(s < vals[-1], n * count_scale, count)


def _peer_select_kernel(q1_ref, q2_ref, k1_ref, k2_ref, cnt_ref, e1_ref, rank_ref, e2_ref, *, tb):
    s1_all = _nt_dot(k1_ref[...], q1_ref[0])
    s2_all = _nt_dot(k2_ref[...], q2_ref[0])
    for l in range(tb // LANES):
        sl = slice(LANES * l, LANES * (l + 1))
        s1 = s1_all[:, sl]
        s2 = s2_all[:, sl]
        v1 = _top_values(s1, N_RANK)
        v2, rank = _top_values(s2, N_RANK, RANK_SCALE)
        pad = [jnp.full((1, LANES), NEG_INF, F32)] * (3 * SUBLANES - N_RANK)
        v1_col = jnp.concatenate(v1 + pad, axis=0)
        v2_col = jnp.concatenate(v2 + pad, axis=0)
        cands = ([v1[0] + v2_col] + [v1[a] + v2_col[0:SUBLANES] for a in range(1, SUBLANES)]
                 + [v1_col[SUBLANES:] + v2[0]])
        cand = jnp.concatenate(cands, axis=0)
        c = _top_values(cand, N_RANK)
        theta = 0.5 * (c[PEER_TOPK - 1] + c[PEER_TOPK])
        z = jnp.sum(jnp.where(cand >= theta, jnp.exp(cand - c[0]), 0.0), axis=0, keepdims=True)
        thr = theta - s1
        cnt = jnp.zeros_like(s1)
        for b in range(PEER_TOPK):
            cnt = jnp.where(v2[b] >= thr, (b + 1) * RANK_SCALE, cnt)
        cnt_ref[0, :, sl] = cnt
        e1_ref[0, :, sl] = jnp.exp(s1 - v1[0]) / z
        rank_ref[0, l] = rank.astype(BF16).reshape(KEY_GROUPS, BF16_ROWS, LANES)
        e2_ref[0, l] = jnp.exp(s2 - v2[0]).astype(BF16).reshape(KEY_GROUPS, BF16_ROWS, LANES)


def _peer_select(q, k1, k2, tb):
    t = q.shape[1]
    rows = jax.ShapeDtypeStruct((PEER_HEADS, N_KEYS, t), F32)
    rows_spec = pl.BlockSpec((1, N_KEYS, tb), lambda i, h: (h, 0, i))
    tiles = jax.ShapeDtypeStruct((PEER_HEADS, t // LANES, KEY_GROUPS, BF16_ROWS, LANES), BF16)
    tiles_spec = pl.BlockSpec((1, tb // LANES, KEY_GROUPS, BF16_ROWS, LANES), lambda i, h: (h, i, 0, 0, 0))
    return pl.pallas_call(
        functools.partial(_peer_select_kernel, tb=tb),
        grid=(t // tb, PEER_HEADS),
        in_specs=[
            pl.BlockSpec((1, tb, D_HALF), lambda i, h: (2 * h, i, 0)),
            pl.BlockSpec((1, tb, D_HALF), lambda i, h: (2 * h + 1, i, 0)),
            pl.BlockSpec((N_KEYS, D_HALF), lambda i, h: (0, 0)),
            pl.BlockSpec((N_KEYS, D_HALF), lambda i, h: (0, 0)),
        ],
        out_specs=[rows_spec, rows_spec, tiles_spec, tiles_spec],
        out_shape=[rows, rows, tiles, tiles],
        compiler_params=pltpu.CompilerParams(
            dimension_semantics=("arbitrary", "arbitrary"), vmem_limit_bytes=VMEM_LIMIT),
        name="peer_select",
    )(q, q, k1, k2)


def _peer_dense_kernel(u_ref, vt_ref, xnt_ref, x1_ref, cnt_ref, e1_ref, rank_ref, e2_ref, g_ref,
                       ya_ref, yb_ref, act0_ref, act1_ref, wt0_ref, wt1_ref, acc_ref, *, tb, eb, ne, n_a,
                       strip_rows):
    s = pl.program_id(0)
    nl = tb // LANES

    @pl.when(s == 0)
    def _():
        act1_ref[...] = jnp.zeros_like(act1_ref)
        wt0_ref[...] = jnp.zeros_like(wt0_ref)
        acc_ref[...] = jnp.zeros_like(acc_ref)

    assert eb // N_KEYS == SUBLANES
    assert nl % 2 == 0
    pieces_per_half = 2 * SUBLANES
    c_every = pieces_per_half * strip_rows // D_MODEL
    a_every = pieces_per_half * strip_rows // eb
    assert c_every >= 2 and a_every > c_every // 2
    pack =lambda row: jnp.broadcast_to(row, (BF16_ROWS, LANES)).astype(BF16)[None]

    def stages(p):
        act_w, act_r = (act0_ref, act1_ref) if p == 0 else (act1_ref, act0_ref)
        wt_r, wt_w = (wt0_ref, wt1_ref) if p == 0 else (wt1_ref, wt0_ref)
        piece = 0
        for half in range(nl // 2):
            cols = slice(2 * LANES * half, 2 * LANES * (half + 1))
            wt_cols = jnp.concatenate([wt_r[2 * half], wt_r[2 * half + 1]], axis=1)
            for l in (2 * half, 2 * half + 1):
                sl = slice(LANES * l, LANES * (l + 1))
                for ib in range(SUBLANES):
                    rows = slice(N_KEYS * ib, N_KEYS * (ib + 1))
                    k = piece % pieces_per_half
                    if k % c_every == 0:
                        j = k // c_every
                        orows = slice(strip_rows * j, strip_rows * (j + 1))
                        acc_ref[orows, cols] += jnp.dot(vt_ref[0, orows, :], wt_cols,
                                                        preferred_element_type=F32)
                    if k % a_every == c_every // 2:
                        a = k // a_every
                        arows = slice(strip_rows * a, strip_rows * (a + 1))
                        strip = jnp.dot(u_ref[arows, :], xnt_ref[:, cols], preferred_element_type=F32)
                        act_w[2 * half, arows, :] = strip[:, 0:LANES]
                        act_w[2 * half + 1, arows, :] = strip[:, LANES:2 * LANES]
                    w = jnp.zeros((KEY_GROUPS, BF16_ROWS, LANES), BF16)
                    for h in range(PEER_HEADS):
                        cnt = pack(cnt_ref[h, ib:ib + 1, sl])
                        e1 = pack(e1_ref[h, ib:ib + 1, sl])
                        w = w + jnp.maximum(jnp.minimum(e1 * e2_ref[h, l], cnt - rank_ref[h, l]), 0.0)
                    g = jax.nn.gelu(act_r[l, rows, :]).astype(BF16).reshape(KEY_GROUPS, BF16_ROWS, LANES)
                    wt_w[l, rows, :] = (w * g).reshape(N_KEYS, LANES)
                    piece += 1

    for parity in range(2):
        pl.when(s % 2 == parity)(functools.partial(stages, parity))

    @pl.when(jnp.logical_and((s - 1) % ne == 0, s > 1))
    def _():
        x2 = x1_ref[...] + acc_ref[...].T
        acc_ref[...] = jnp.zeros_like(acc_ref)
        y = _rms(x2, g_ref[...])
        done_tile = (s - 2) // ne

        @pl.when(done_tile < n_a)
        def _():
            ya_ref[...] = y

        @pl.when(done_tile >= n_a)
        def _():
            yb_ref[...] = y


def _peer_dense(u, vt, xnt, x1, cnt, e1, rank, e2, g, tb, eb, n_a, strip_rows):
    t = x1.shape[0]
    nt = t // tb
    n_b = nt - n_a
    ne = N_EXPERTS // eb
    tile = lambda s, lag: jnp.clip((s - lag) // ne, 0, nt - 1)
    rows_spec = pl.BlockSpec((PEER_HEADS, SUBLANES, tb), lambda s: (0, (s + ne - 1) % ne, tile(s, 1)))
    tiles_spec = pl.BlockSpec((PEER_HEADS, tb // LANES, KEY_GROUPS, BF16_ROWS, LANES),
                              lambda s: (0, tile(s, 1), 0, 0, 0))
    act_buf = pltpu.VMEM((tb // LANES, eb, LANES), F32)
    wt_buf = pltpu.VMEM((tb // LANES, eb, LANES), BF16)
    return pl.pallas_call(
        functools.partial(_peer_dense_kernel, tb=tb, eb=eb, ne=ne, n_a=n_a, strip_rows=strip_rows),
        grid=(nt * ne + 2,),
        in_specs=[
            pl.BlockSpec((eb, D_MODEL), lambda s: (s % ne, 0)),
            pl.BlockSpec((1, D_MODEL, eb), lambda s: ((s + ne - 2) % ne, 0, 0)),
            pl.BlockSpec((D_MODEL, tb), lambda s: (0, tile(s, 0))),
            pl.BlockSpec((tb, D_MODEL), lambda s: (tile(s, 2), 0), pipeline_mode=pl.Buffered(1)),
            rows_spec, rows_spec, tiles_spec, tiles_spec,
            pl.BlockSpec((1, D_MODEL), lambda s: (0, 0)),
        ],
        out_specs=[
            pl.BlockSpec((tb, D_MODEL), lambda s: (jnp.minimum(tile(s, 2), n_a - 1), 0)),
            pl.BlockSpec((tb, D_MODEL), lambda s: (jnp.clip(tile(s, 2) - n_a, 0, n_b - 1), 0)),
        ],
        out_shape=[jax.ShapeDtypeStruct((n_a * tb, D_MODEL), F32),
                   jax.ShapeDtypeStruct((n_b * tb, D_MODEL), F32)],
        scratch_shapes=[act_buf, act_buf, wt_buf, wt_buf, pltpu.VMEM((D_MODEL, tb), F32)],
        compiler_params=pltpu.CompilerParams(
            dimension_semantics=("arbitrary",), vmem_limit_bytes=VMEM_LIMIT),
        name="peer_dense",
    )(u, vt, xnt, x1, cnt, e1, rank, e2, g)


def _gate_weights(w_a, w_x):
    def blockdiag(w):
        w = w.reshape(LRU_HEADS // 2, 2, LRU_HEAD_DIM, LRU_HEAD_DIM)
        zero = jnp.zeros_like(w[:, 0])
        top = jnp.concatenate([w[:, 0], zero], axis=-1)
        bot = jnp.concatenate([zero, w[:, 1]], axis=-1)
        return jnp.concatenate([top, bot], axis=-2)
    return jnp.concatenate([blockdiag(w_a), blockdiag(w_x)], axis=-1).astype(BF16)


def _right_align(cache, rows):
    n, w, c = cache.shape
    return jnp.concatenate([jnp.zeros((n, rows - w, c), cache.dtype), cache], axis=1)


def kernel(x_prompt, x_sample, state_lru, cache_conv_lru, cache_conv_short, g_mix, w_in, conv_lru_w,
           conv_lru_b, lru_w_a, lru_b_a, lru_w_x, lru_b_x, lru_lambda, conv_sc_w, g_out_lru, g_out_sc,
           w_out, g_ffn, peer_w_q, peer_k1, peer_k2, peer_u, peer_v, g_final):
    depth = g_mix.shape[0]
    assert depth == 1, "the final rmsnorm is fused into the (single) layer's PEER kernel"
    nb, seq, _ = x_prompt.shape
    ndec, dseq, _ = x_sample.shape
    n_prompt = nb * seq
    xa = x_prompt.reshape(n_prompt, D_MODEL)
    xb = x_sample.reshape(ndec * dseq, D_MODEL)

    outs = {k: [] for k in ("hp", "c4p", "c3p", "hs", "c4s", "c3s")}
    for l in range(depth):
        row = lambda a: a[l].reshape(1, -1)
        z = _norm_matmul(xa, xb, row(g_mix), w_in[l].astype(BF16), tm=512)

        params = (conv_lru_w[l], row(conv_lru_b), _gate_weights(lru_w_a[l], lru_w_x[l]), row(lru_b_a),
                  row(lru_b_x), row(lru_lambda), conv_sc_w[l], row(g_out_lru), row(g_out_sc))
        o_p, tail_p = _mixer(
            z, 0, nb, seq, 256, jnp.zeros((nb, 1, D_LRU), F32), jnp.zeros((nb, SUBLANES, D_LRU), F32),
            jnp.zeros((nb, SUBLANES, D_SC), F32), params, True)
        o_s, tail_s = _mixer(
            z, n_prompt, ndec, dseq, dseq, state_lru[l].reshape(ndec, 1, D_LRU),
            _right_align(cache_conv_lru[l], SUBLANES), _right_align(cache_conv_short[l], SUBLANES),
            params, False)
        x1, xnt, q = _out_q(o_p, o_s, xa, xb, w_out[l].astype(BF16), row(g_ffn), peer_w_q[l].astype(BF16),
                            tm=512)
        cnt, e1, rank, e2 = _peer_select(q, peer_k1[l].astype(BF16), peer_k2[l].astype(BF16), tb=512)
        eb = SUBLANES * N_KEYS
        vt = peer_v[l].reshape(N_EXPERTS // eb, eb, D_MODEL).transpose(0, 2, 1).astype(BF16)
        xa, xb = _peer_dense(peer_u[l].astype(BF16), vt, xnt, x1, cnt, e1, rank, e2,
                             g_final.reshape(1, -1), tb=512, eb=eb, n_a=n_prompt // 512,
                             strip_rows=256)

        for name, tail, i, w in (("hp", tail_p, 0, 1), ("c4p", tail_p, 1, LRU_CONV - 1),
                                 ("c3p", tail_p, 2, SC_CONV - 1), ("hs", tail_s, 0, 1),
                                 ("c4s", tail_s, 1, LRU_CONV - 1), ("c3s", tail_s, 2, SC_CONV - 1)):
            outs[name].append(tail[:, i, SUBLANES - w:, :])

    y_prompt = xa.reshape(x_prompt.shape)
    y_sample = xb.reshape(x_sample.shape)
    hp = jnp.stack(outs["hp"])[:, :, 0, :]
    hs = jnp.stack(outs["hs"])[:, :, 0, :]
    return (y_prompt, y_sample, hp, jnp.stack(outs["c4p"]), jnp.stack(outs["c3p"]),
            hs, jnp.stack(outs["c4s"]), jnp.stack(outs["c3s"]))
```

```python
import functools

import jax
import jax.numpy as jnp
from jax import lax
from jax.experimental import pallas as pl
from jax.experimental.pallas import tpu as pltpu

D_MODEL = 2048
D_LRU = 1024
D_SC = 1024
D_IN = 2 * D_LRU + 3 * D_SC
LRU_HEADS = 16
LRU_HEAD_DIM = D_LRU // LRU_HEADS
LRU_CONV = 4
SC_CONV = 3
LRU_C = 8.0
N_KEYS = 128
N_EXPERTS = N_KEYS * N_KEYS
PEER_HEADS = 8
PEER_TOPK = 16
D_HALF = 128
EPS = 1e-6

LANES = 128
SUBLANES = 8
VMEM_LIMIT = 60 * 1024 * 1024

F32 = jnp.float32
BF16 = jnp.bfloat16
NEG_INF = float("-inf")


def _rms(x, g):
    ms = jnp.mean(x * x, axis=-1, keepdims=True)
    return x * lax.rsqrt(ms + EPS) * g


EXPM1_SERIES_TERMS = 9
EXPM1_SERIES_BOUND = 0.35


def _neg_expm1(x, exp_x):
    s = 1.0 + x * (1.0 / EXPM1_SERIES_TERMS)
    for n in range(EXPM1_SERIES_TERMS - 1, 1, -1):
        s = 1.0 + (x * (1.0 / n)) * s
    return jnp.where(x > -EXPM1_SERIES_BOUND, -x * s, 1.0 - exp_x)


def _nt_dot(a, b):
    return lax.dot_general(a, b, (((1,), (1,)), ((), ())), preferred_element_type=F32)


def _two_group_specs(n_a, n_b, tm, width):
    return [pl.BlockSpec((tm, width), lambda i, *_: (jnp.minimum(i, n_a - 1), 0)),
            pl.BlockSpec((tm, width), lambda i, *_: (jnp.clip(i - n_a, 0, n_b - 1), 0),
                         pipeline_mode=pl.Buffered(1))]


def _norm_matmul_kernel(xa_ref, xb_ref, g_ref, w_ref, o_ref, *, n_a):
    def project(x_ref):
        xn = _rms(x_ref[...], g_ref[...]).astype(BF16)
        o_ref[...] = jnp.dot(xn, w_ref[...], preferred_element_type=F32).astype(o_ref.dtype)

    in_a = pl.program_id(0) < n_a
    pl.when(in_a)(functools.partial(project, xa_ref))
    pl.when(jnp.logical_not(in_a))(functools.partial(project, xb_ref))


def _norm_matmul(xa, xb, g, w, tm):
    d = xa.shape[1]
    n_a, n_b = xa.shape[0] // tm, xb.shape[0] // tm
    t = xa.shape[0] + xb.shape[0]
    n = w.shape[1]
    return pl.pallas_call(
        functools.partial(_norm_matmul_kernel, n_a=n_a),
        grid=(t // tm,),
        in_specs=_two_group_specs(n_a, n_b, tm, d) + [
            pl.BlockSpec((1, d), lambda i: (0, 0)),
            pl.BlockSpec((d, n), lambda i: (0, 0), pipeline_mode=pl.Buffered(1)),
        ],
        out_specs=pl.BlockSpec((tm, n), lambda i: (i, 0)),
        out_shape=jax.ShapeDtypeStruct((t, n), BF16),
        compiler_params=pltpu.CompilerParams(
            dimension_semantics=("arbitrary",), vmem_limit_bytes=VMEM_LIMIT),
        name="norm_in_proj",
    )(xa, xb, g, w)


SCAN_UNROLL = 4


def _mixer_kernel(z_ref, h0_ref, c4_ref, c3_ref, cw_ref, cb_ref, wg_ref, ba_ref, bx_ref, lam_ref,
                  sw_ref, gl_ref, gs_ref, o_ref, tail_ref,
                  xbuf, ubuf, a_s, b_s, h_s, hcar, *, tt, reset_first):
    t = pl.program_id(1)

    @pl.when(t == 0)
    def _():
        xbuf[0:SUBLANES, :] = c4_ref[0]
        ubuf[0:SUBLANES, :] = c3_ref[0]
        hcar[...] = jnp.broadcast_to(h0_ref[0], (SUBLANES, D_LRU))

    @pl.when(t > 0)
    def _():
        xbuf[0:SUBLANES, :] = xbuf[tt:tt + SUBLANES, :]
        ubuf[0:SUBLANES, :] = ubuf[tt:tt + SUBLANES, :]

    xbuf[SUBLANES:SUBLANES + tt, :] = z_ref[:, 0:D_LRU].astype(F32)
    ubuf[SUBLANES:SUBLANES + tt, :] = (z_ref[:, 2 * D_LRU + D_SC:2 * D_LRU + 2 * D_SC].astype(F32)
                                       * z_ref[:, 2 * D_LRU + 2 * D_SC:D_IN].astype(F32))

    nlam = -lam_ref[...]
    sp = jnp.maximum(nlam, 0.0) + jnp.log1p(jnp.exp(-jnp.abs(nlam)))

    row = lax.broadcasted_iota(jnp.int32, (tt, LANES), 0)
    for j in range(D_LRU // LANES):
        sl = slice(LANES * j, LANES * (j + 1))
        base = SUBLANES - (LRU_CONV - 1)
        xc = xbuf[base:base + tt, sl] * cw_ref[0:1, sl]
        for k in range(1, LRU_CONV):
            xc = xc + xbuf[base + k:base + k + tt, sl] * cw_ref[k:k + 1, sl]
        xc = xc + cb_ref[:, sl]
        gates = jnp.dot(xc.astype(BF16), wg_ref[j], preferred_element_type=F32)
        r = jax.nn.sigmoid(gates[:, 0:LANES] + ba_ref[:, sl])
        i = jax.nn.sigmoid(gates[:, LANES:2 * LANES] + bx_ref[:, sl])
        log_a = (-LRU_C * r) * sp[:, sl]
        a = jnp.exp(log_a)
        mult = jnp.sqrt(_neg_expm1(2.0 * log_a, a * a))
        if reset_first:
            mult = jnp.where(jnp.logical_and(row == 0, t == 0), 1.0, mult)
        a_s[:, sl] = a
        b_s[:, sl] = mult * (i * xc)

    srow = lax.broadcasted_iota(jnp.int32, (SUBLANES, D_LRU), 0)

    def group(gi, h_before):
        r0 = pl.multiple_of(gi * SUBLANES, SUBLANES)
        av = a_s[pl.ds(r0, SUBLANES), :]
        bv = b_s[pl.ds(r0, SUBLANES), :]
        for d in (1, 2, 4):
            keep = srow >= d
            a_sh = jnp.where(keep, pltpu.roll(av, d, 0), 1.0)
            b_sh = jnp.where(keep, pltpu.roll(bv, d, 0), 0.0)
            bv = av * b_sh + bv
            av = av * a_sh
        hprev = jnp.broadcast_to(h_before[SUBLANES - 1:SUBLANES, :], (SUBLANES, D_LRU))
        hv = bv + av * hprev
        h_s[pl.ds(r0, SUBLANES), :] = hv
        return hv

    hcar[...] = lax.fori_loop(0, tt // SUBLANES, group, hcar[...], unroll=SCAN_UNROLL)

    rc = min(tt, 32)
    for c in range(tt // rc):
        rs = slice(rc * c, rc * (c + 1))
        o_lru = h_s[rs, :] * jax.nn.gelu(z_ref[rs, D_LRU:2 * D_LRU].astype(F32))
        o_ref[rs, 0:D_LRU] = _rms(o_lru, gl_ref[...]).astype(BF16)
        base = SUBLANES - (SC_CONV - 1) + rc * c
        uc = ubuf[base:base + rc, :] * sw_ref[0:1, :]
        for k in range(1, SC_CONV):
            uc = uc + ubuf[base + k:base + k + rc, :] * sw_ref[k:k + 1, :]
        o_sc = z_ref[rs, 2 * D_LRU:2 * D_LRU + D_SC].astype(F32) * uc
        o_ref[rs, D_LRU:D_LRU + D_SC] = _rms(o_sc, gs_ref[...]).astype(BF16)

    tail_ref[0, 0] = h_s[tt - SUBLANES:tt, :]
    tail_ref[0, 1] = xbuf[tt:tt + SUBLANES, :]
    tail_ref[0, 2] = ubuf[tt:tt + SUBLANES, :]


def _mixer(z, row0, nseq, tseq, tt, h0, c4, c3, params, reset_first):
    nt = tseq // tt
    blk0 = row0 // tt
    cw, cb, wg, ba, bx, lam, sw, gl, gs = params
    full = lambda a: pl.BlockSpec(a.shape, lambda s, t: (0,) * a.ndim)
    return pl.pallas_call(
        functools.partial(_mixer_kernel, tt=tt, reset_first=reset_first),
        grid=(nseq, nt),
        in_specs=[
            pl.BlockSpec((tt, D_IN), lambda s, t: (blk0 + s * nt + t, 0)),
            pl.BlockSpec((1, 1, D_LRU), lambda s, t: (s, 0, 0)),
            pl.BlockSpec((1, SUBLANES, D_LRU), lambda s, t: (s, 0, 0)),
            pl.BlockSpec((1, SUBLANES, D_SC), lambda s, t: (s, 0, 0)),
            full(cw), full(cb), full(wg), full(ba), full(bx), full(lam), full(sw), full(gl), full(gs),
        ],
        out_specs=[
            pl.BlockSpec((tt, D_LRU + D_SC), lambda s, t: (s * nt + t, 0)),
            pl.BlockSpec((1, 3, SUBLANES, D_LRU), lambda s, t: (s, 0, 0, 0)),
        ],
        out_shape=[
            jax.ShapeDtypeStruct((nseq * tseq, D_LRU + D_SC), BF16),
            jax.ShapeDtypeStruct((nseq, 3, SUBLANES, D_LRU), F32),
        ],
        scratch_shapes=[
            pltpu.VMEM((tt + 2 * SUBLANES, D_LRU), F32),
            pltpu.VMEM((tt + 2 * SUBLANES, D_SC), F32),
            pltpu.VMEM((tt, D_LRU), F32),
            pltpu.VMEM((tt, D_LRU), F32),
            pltpu.VMEM((tt, D_LRU), F32),
            pltpu.VMEM((SUBLANES, D_LRU), F32),
        ],
        compiler_params=pltpu.CompilerParams(
            dimension_semantics=("arbitrary", "arbitrary"), vmem_limit_bytes=VMEM_LIMIT),
        name="mixer_reset" if reset_first else "mixer_cont",
    )(z, h0, c4, c3, cw, cb, wg, ba, bx, lam, sw, gl, gs)


def _out_q_kernel(oa_ref, ob_ref, xa_ref, xb_ref, wout_ref, g_ref, wq_ref, x1_ref, xnt_ref, q_ref,
                  *, n_a):
    in_a = pl.program_id(0) < n_a

    @pl.when(in_a)
    def _():
        x1_ref[...] = xa_ref[...] + jnp.dot(oa_ref[...], wout_ref[...], preferred_element_type=F32)

    @pl.when(jnp.logical_not(in_a))
    def _():
        x1_ref[...] = xb_ref[...] + jnp.dot(ob_ref[...], wout_ref[...], preferred_element_type=F32)

    x1 = x1_ref[...]
    xn32 = _rms(x1, g_ref[...])
    xn = xn32.astype(BF16)
    xnt_ref[...] = xn32.T.astype(BF16)
    q = jnp.dot(xn, wq_ref[...], preferred_element_type=F32).astype(BF16)
    for c in range(2 * PEER_HEADS):
        q_ref[c] = q[:, D_HALF * c:D_HALF * (c + 1)]


def _out_q(oa, ob, xa, xb, w_out, g, w_q, tm):
    n_a, n_b = xa.shape[0] // tm, xb.shape[0] // tm
    t = xa.shape[0] + xb.shape[0]
    const = lambda a: pl.BlockSpec(a.shape, lambda i: (0,) * a.ndim, pipeline_mode=pl.Buffered(1))
    return pl.pallas_call(
        functools.partial(_out_q_kernel, n_a=n_a),
        grid=(t // tm,),
        in_specs=_two_group_specs(n_a, n_b, tm, D_MODEL) + _two_group_specs(n_a, n_b, tm, D_MODEL) + [
            const(w_out), const(g), const(w_q),
        ],
        out_specs=[
            pl.BlockSpec((tm, D_MODEL), lambda i: (i, 0)),
            pl.BlockSpec((D_MODEL, tm), lambda i: (0, i)),
            pl.BlockSpec((2 * PEER_HEADS, tm, D_HALF), lambda i: (0, i, 0)),
        ],
        out_shape=[
            jax.ShapeDtypeStruct((t, D_MODEL), F32),
            jax.ShapeDtypeStruct((D_MODEL, t), BF16),
            jax.ShapeDtypeStruct((2 * PEER_HEADS, t, D_HALF), BF16),
        ],
        compiler_params=pltpu.CompilerParams(
            dimension_semantics=("arbitrary",), vmem_limit_bytes=VMEM_LIMIT),
        name="out_proj_q_proj",
    )(oa, ob, xa, xb, w_out, g, w_q)


N_RANK = PEER_TOPK + 1
BF16_ROWS = 2 * SUBLANES
RANK_SCALE = 2.0 ** 20
KEY_GROUPS = N_KEYS // BF16_ROWS


def _top_values(s, n, count_scale=None):
    vals = [jnp.max(s, axis=0, keepdims=True)]
    count = jnp.zeros_like(s)
    for r in range(1, n):
        below = s < vals[-1]
        if count_scale is not None:
            count = jnp.where(below, r * count_scale, count)
        vals.append(jnp.max(jnp.where(below, s, NEG_INF), axis=0, keepdims=True))
    if count_scale is None:
        return vals
    return vals, jnp.where(s < vals[-1], n * count_scale, count)


def _peer_select_kernel(q1_ref, q2_ref, k1_ref, k2_ref, cnt_ref, e1_ref, rank_ref, e2_ref, *, tb):
    s1_all = _nt_dot(k1_ref[...], q1_ref[0])
    s2_all = _nt_dot(k2_ref[...], q2_ref[0])
    for l in range(tb // LANES):
        sl = slice(LANES * l, LANES * (l + 1))
        s1 = s1_all[:, sl]
        s2 = s2_all[:, sl]
        v1 = _top_values(s1, N_RANK)
        v2, rank = _top_values(s2, N_RANK, RANK_SCALE)
        pad = [jnp.full((1, LANES), NEG_INF, F32)] * (3 * SUBLANES - N_RANK)
        v1_col = jnp.concatenate(v1 + pad, axis=0)
        v2_col = jnp.concatenate(v2 + pad, axis=0)
        cands = ([v1[0] + v2_col] + [v1[a] + v2_col[0:SUBLANES] for a in range(1, SUBLANES)]
                 + [v1_col[SUBLANES:] + v2[0]])
        cand = jnp.concatenate(cands, axis=0)
        c = _top_values(cand, N_RANK)
        theta = 0.5 * (c[PEER_TOPK - 1] + c[PEER_TOPK])
        z = jnp.sum(jnp.where(cand >= theta, jnp.exp(cand - c[0]), 0.0), axis=0, keepdims=True)
        thr = theta - s1
        cnt = jnp.zeros_like(s1)
        for b in range(PEER_TOPK):
            cnt = jnp.where(v2[b] >= thr, (b + 1) * RANK_SCALE, cnt)
        cnt_ref[0, :, sl] = cnt
        e1_ref[0, :, sl] = jnp.exp(s1 - v1[0]) / z
        rank_ref[0, l] = rank.astype(BF16).reshape(KEY_GROUPS, BF16_ROWS, LANES)
        e2_ref[0, l] = jnp.exp(s2 - v2[0]).astype(BF16).reshape(KEY_GROUPS, BF16_ROWS, LANES)


def _peer_select(q, k1, k2, tb):
    t = q.shape[1]
    rows = jax.ShapeDtypeStruct((PEER_HEADS, N_KEYS, t), F32)
    rows_spec = pl.BlockSpec((1, N_KEYS, tb), lambda i, h: (h, 0, i))
    tiles = jax.ShapeDtypeStruct((PEER_HEADS, t // LANES, KEY_GROUPS, BF16_ROWS, LANES), BF16)
    tiles_spec = pl.BlockSpec((1, tb // LANES, KEY_GROUPS, BF16_ROWS, LANES), lambda i, h: (h, i, 0, 0, 0))
    return pl.pallas_call(
        functools.partial(_peer_select_kernel, tb=tb),
        grid=(t // tb, PEER_HEADS),
        in_specs=[
            pl.BlockSpec((1, tb, D_HALF), lambda i, h: (2 * h, i, 0)),
            pl.BlockSpec((1, tb, D_HALF), lambda i, h: (2 * h + 1, i, 0)),
            pl.BlockSpec((N_KEYS, D_HALF), lambda i, h: (0, 0)),
            pl.BlockSpec((N_KEYS, D_HALF), lambda i, h: (0, 0)),
        ],
        out_specs=[rows_spec, rows_spec, tiles_spec, tiles_spec],
        out_shape=[rows, rows, tiles, tiles],
        compiler_params=pltpu.CompilerParams(
            dimension_semantics=("arbitrary", "arbitrary"), vmem_limit_bytes=VMEM_LIMIT),
        name="peer_select",
    )(q, q, k1, k2)


def _peer_dense_kernel(u_ref, vt_ref, xnt_ref, x1_ref, cnt_ref, e1_ref, rank_ref, e2_ref, g_ref,
                       ya_ref, yb_ref, act0_ref, act1_ref, wt0_ref, wt1_ref, acc_ref, *, tb, eb, ne, n_a,
                       strip_rows, col_width):
    s = pl.program_id(0)
    nl = tb // LANES

    @pl.when(s == 0)
    def _():
        act1_ref[...] = jnp.zeros_like(act1_ref)
        wt0_ref[...] = jnp.zeros_like(wt0_ref)
        acc_ref[...] = jnp.zeros_like(acc_ref)

    assert eb // N_KEYS == SUBLANES
    gl = col_width // LANES
    assert nl % gl == 0
    pieces_per_group = SUBLANES * gl
    c_every = pieces_per_group * strip_rows // D_MODEL
    a_every = pieces_per_group * strip_rows // eb
    assert c_every >= 2 and a_every > c_every // 2
    pack =lambda row: jnp.broadcast_to(row, (BF16_ROWS, LANES)).astype(BF16)[None]

    def stages(p):
        act_w, act_r = (act0_ref, act1_ref) if p == 0 else (act1_ref, act0_ref)
        wt_r, wt_w = (wt0_ref, wt1_ref) if p == 0 else (wt1_ref, wt0_ref)
        piece = 0
        for grp in range(nl // gl):
            lanes = range(gl * grp, gl * (grp + 1))
            cols = slice(LANES * gl * grp, LANES * gl * (grp + 1))
            wt_cols = jnp.concatenate([wt_r[l] for l in lanes], axis=1)
            for l in lanes:
                sl = slice(LANES * l, LANES * (l + 1))
                for ib in range(SUBLANES):
                    rows = slice(N_KEYS * ib, N_KEYS * (ib + 1))
                    k = piece % pieces_per_group
                    if k % c_every == 0:
                        j = k // c_every
                        orows = slice(strip_rows * j, strip_rows * (j + 1))
                        acc_ref[orows, cols] += jnp.dot(vt_ref[0, orows, :], wt_cols,
                                                        preferred_element_type=F32)
                    if k % a_every == c_every // 2:
                        a = k // a_every
                        arows = slice(strip_rows * a, strip_rows * (a + 1))
                        strip = jnp.dot(u_ref[arows, :], xnt_ref[:, cols], preferred_element_type=F32)
                        for n, ll in enumerate(lanes):
                            act_w[ll, arows, :] = strip[:, LANES * n:LANES * (n + 1)]
                    w = jnp.zeros((KEY_GROUPS, BF16_ROWS, LANES), BF16)
                    for h in range(PEER_HEADS):
                        cnt = pack(cnt_ref[h, ib:ib + 1, sl])
                        e1 = pack(e1_ref[h, ib:ib + 1, sl])
                        w = w + jnp.maximum(jnp.minimum(e1 * e2_ref[h, l], cnt - rank_ref[h, l]), 0.0)
                    g = jax.nn.gelu(act_r[l, rows, :]).astype(BF16).reshape(KEY_GROUPS, BF16_ROWS, LANES)
                    wt_w[l, rows, :] = (w * g).reshape(N_KEYS, LANES)
                    piece += 1

    for parity in range(2):
        pl.when(s % 2 == parity)(functools.partial(stages, parity))

    @pl.when(jnp.logical_and((s - 1) % ne == 0, s > 1))
    def _():
        x2 = x1_ref[...] + acc_ref[...].T
        acc_ref[...] = jnp.zeros_like(acc_ref)
        y = _rms(x2, g_ref[...])
        done_tile = (s - 2) // ne

        @pl.when(done_tile < n_a)
        def _():
            ya_ref[...] = y

        @pl.when(done_tile >= n_a)
        def _():
            yb_ref[...] = y


def _peer_dense(u, vt, xnt, x1, cnt, e1, rank, e2, g, tb, eb, n_a, strip_rows, col_width):
    t = x1.shape[0]
    nt = t // tb
    n_b = nt - n_a
    ne = N_EXPERTS // eb
    tile = lambda s, lag: jnp.clip((s - lag) // ne, 0, nt - 1)
    rows_spec = pl.BlockSpec((PEER_HEADS, SUBLANES, tb), lambda s: (0, (s + ne - 1) % ne, tile(s, 1)))
    tiles_spec = pl.BlockSpec((PEER_HEADS, tb // LANES, KEY_GROUPS, BF16_ROWS, LANES),
                              lambda s: (0, tile(s, 1), 0, 0, 0))
    act_buf = pltpu.VMEM((tb // LANES, eb, LANES), F32)
    wt_buf = pltpu.VMEM((tb // LANES, eb, LANES), BF16)
    return pl.pallas_call(
        functools.partial(_peer_dense_kernel, tb=tb, eb=eb, ne=ne, n_a=n_a, strip_rows=strip_rows,
                          col_width=col_width),
        grid=(nt * ne + 2,),
        in_specs=[
            pl.BlockSpec((eb, D_MODEL), lambda s: (s % ne, 0)),
            pl.BlockSpec((1, D_MODEL, eb), lambda s: ((s + ne - 2) % ne, 0, 0)),
            pl.BlockSpec((D_MODEL, tb), lambda s: (0, tile(s, 0))),
            pl.BlockSpec((tb, D_MODEL), lambda s: (tile(s, 2), 0), pipeline_mode=pl.Buffered(1)),
            rows_spec, rows_spec, tiles_spec, tiles_spec,
            pl.BlockSpec((1, D_MODEL), lambda s: (0, 0)),
        ],
        out_specs=[
            pl.BlockSpec((tb, D_MODEL), lambda s: (jnp.minimum(tile(s, 2), n_a - 1), 0)),
            pl.BlockSpec((tb, D_MODEL), lambda s: (jnp.clip(tile(s, 2) - n_a, 0, n_b - 1), 0)),
        ],
        out_shape=[jax.ShapeDtypeStruct((n_a * tb, D_MODEL), F32),
                   jax.ShapeDtypeStruct((n_b * tb, D_MODEL), F32)],
        scratch_shapes=[act_buf, act_buf, wt_buf, wt_buf, pltpu.VMEM((D_MODEL, tb), F32)],
        compiler_params=pltpu.CompilerParams(
            dimension_semantics=("arbitrary",), vmem_limit_bytes=VMEM_LIMIT),
        name="peer_dense",
    )(u, vt, xnt, x1, cnt, e1, rank, e2, g)


def _gate_weights(w_a, w_x):
    def blockdiag(w):
        w = w.reshape(LRU_HEADS // 2, 2, LRU_HEAD_DIM, LRU_HEAD_DIM)
        zero = jnp.zeros_like(w[:, 0])
        top = jnp.concatenate([w[:, 0], zero], axis=-1)
        bot = jnp.concatenate([zero, w[:, 1]], axis=-1)
        return jnp.concatenate([top, bot], axis=-2)
    return jnp.concatenate([blockdiag(w_a), blockdiag(w_x)], axis=-1).astype(BF16)


def _right_align(cache, rows):
    n, w, c = cache.shape
    return jnp.concatenate([jnp.zeros((n, rows - w, c), cache.dtype), cache], axis=1)


def kernel(x_prompt, x_sample, state_lru, cache_conv_lru, cache_conv_short, g_mix, w_in, conv_lru_w,
           conv_lru_b, lru_w_a, lru_b_a, lru_w_x, lru_b_x, lru_lambda, conv_sc_w, g_out_lru, g_out_sc,
           w_out, g_ffn, peer_w_q, peer_k1, peer_k2, peer_u, peer_v, g_final):
    depth = g_mix.shape[0]
    assert depth == 1, "the final rmsnorm is fused into the (single) layer's PEER kernel"
    nb, seq, _ = x_prompt.shape
    ndec, dseq, _ = x_sample.shape
    n_prompt = nb * seq
    xa = x_prompt.reshape(n_prompt, D_MODEL)
    xb = x_sample.reshape(ndec * dseq, D_MODEL)

    outs = {k: [] for k in ("hp", "c4p", "c3p", "hs", "c4s", "c3s")}
    for l in range(depth):
        row = lambda a: a[l].reshape(1, -1)
        z = _norm_matmul(xa, xb, row(g_mix), w_in[l].astype(BF16), tm=512)

        params = (conv_lru_w[l], row(conv_lru_b), _gate_weights(lru_w_a[l], lru_w_x[l]), row(lru_b_a),
                  row(lru_b_x), row(lru_lambda), conv_sc_w[l], row(g_out_lru), row(g_out_sc))
        o_p, tail_p = _mixer(
            z, 0, nb, seq, 256, jnp.zeros((nb, 1, D_LRU), F32), jnp.zeros((nb, SUBLANES, D_LRU), F32),
            jnp.zeros((nb, SUBLANES, D_SC), F32), params, True)
        o_s, tail_s = _mixer(
            z, n_prompt, ndec, dseq, dseq, state_lru[l].reshape(ndec, 1, D_LRU),
            _right_align(cache_conv_lru[l], SUBLANES), _right_align(cache_conv_short[l], SUBLANES),
            params, False)
        x1, xnt, q = _out_q(o_p, o_s, xa, xb, w_out[l].astype(BF16), row(g_ffn), peer_w_q[l].astype(BF16),
                            tm=512)
        cnt, e1, rank, e2 = _peer_select(q, peer_k1[l].astype(BF16), peer_k2[l].astype(BF16), tb=512)
        eb = SUBLANES * N_KEYS
        vt = peer_v[l].reshape(N_EXPERTS // eb, eb, D_MODEL).transpose(0, 2, 1).astype(BF16)
        xa, xb = _peer_dense(peer_u[l].astype(BF16), vt, xnt, x1, cnt, e1, rank, e2,
                             g_final.reshape(1, -1), tb=512, eb=eb, n_a=n_prompt // 512,
                             strip_rows=256, col_width=256)

        for name, tail, i, w in (("hp", tail_p, 0, 1), ("c4p", tail_p, 1, LRU_CONV - 1),
                                 ("c3p", tail_p, 2, SC_CONV - 1), ("hs", tail_s, 0, 1),
                                 ("c4s", tail_s, 1, LRU_CONV - 1), ("c3s", tail_s, 2, SC_CONV - 1)):
            outs[name].append(tail[:, i, SUBLANES - w:, :])

    y_prompt = xa.reshape(x_prompt.shape)
    y_sample = xb.reshape(x_sample.shape)
    hp = jnp.stack(outs["hp"])[:, :, 0, :]
    hs = jnp.stack(outs["hs"])[:, :, 0, :]
    return (y_prompt, y_sample, hp, jnp.stack(outs["c4p"]), jnp.stack(outs["c3p"]),
            hs, jnp.stack(outs["c4s"]), jnp.stack(outs["c3s"]))
```

```python
import functools

import jax
import jax.numpy as jnp
from jax import lax
from jax.experimental import pallas as pl
from jax.experimental.pallas import tpu as pltpu

D_MODEL = 2048
D_LRU = 1024
D_SC = 1024
D_IN = 2 * D_LRU + 3 * D_SC
LRU_HEADS = 16
LRU_HEAD_DIM = D_LRU // LRU_HEADS
LRU_CONV = 4
SC_CONV = 3
LRU_C = 8.0
N_KEYS = 128
N_EXPERTS = N_KEYS * N_KEYS
PEER_HEADS = 8
PEER_TOPK = 16
D_HALF = 128
EPS = 1e-6

LANES = 128
SUBLANES = 8
VMEM_LIMIT = 60 * 1024 * 1024

F32 = jnp.float32
BF16 = jnp.bfloat16
NEG_INF = float("-inf")


def _rms(x, g):
    ms = jnp.mean(x * x, axis=-1, keepdims=True)
    return x * lax.rsqrt(ms + EPS) * g


EXPM1_SERIES_TERMS = 9
EXPM1_SERIES_BOUND = 0.35


def _neg_expm1(x, exp_x):
    s = 1.0 + x * (1.0 / EXPM1_SERIES_TERMS)
    for n in range(EXPM1_SERIES_TERMS - 1, 1, -1):
        s = 1.0 + (x * (1.0 / n)) * s
    return jnp.where(x > -EXPM1_SERIES_BOUND, -x * s, 1.0 - exp_x)


GELU_C = 0.7978845608028654
GELU_A = 0.044715


def _gelu_tanh(x):
    inner = x * ((2.0 * GELU_C * GELU_A) * (x * x) + 2.0 * GELU_C)
    return x * jax.nn.sigmoid(inner)


def _nt_dot(a, b):
    return lax.dot_general(a, b, (((1,), (1,)), ((), ())), preferred_element_type=F32)


def _two_group_specs(n_a, n_b, tm, width):
    return [pl.BlockSpec((tm, width), lambda i, *_: (jnp.minimum(i, n_a - 1), 0)),
            pl.BlockSpec((tm, width), lambda i, *_: (jnp.clip(i - n_a, 0, n_b - 1), 0),
                         pipeline_mode=pl.Buffered(1))]


def _norm_matmul_kernel(xa_ref, xb_ref, g_ref, w_ref, o_ref, *, n_a):
    def project(x_ref):
        xn = _rms(x_ref[...], g_ref[...]).astype(BF16)
        o_ref[...] = jnp.dot(xn, w_ref[...], preferred_element_type=F32).astype(o_ref.dtype)

    in_a = pl.program_id(0) < n_a
    pl.when(in_a)(functools.partial(project, xa_ref))
    pl.when(jnp.logical_not(in_a))(functools.partial(project, xb_ref))


def _norm_matmul(xa, xb, g, w, tm):
    d = xa.shape[1]
    n_a, n_b = xa.shape[0] // tm, xb.shape[0] // tm
    t = xa.shape[0] + xb.shape[0]
    n = w.shape[1]
    return pl.pallas_call(
        functools.partial(_norm_matmul_kernel, n_a=n_a),
        grid=(t // tm,),
        in_specs=_two_group_specs(n_a, n_b, tm, d) + [
            pl.BlockSpec((1, d), lambda i: (0, 0)),
            pl.BlockSpec((d, n), lambda i: (0, 0), pipeline_mode=pl.Buffered(1)),
        ],
        out_specs=pl.BlockSpec((tm, n), lambda i: (i, 0)),
        out_shape=jax.ShapeDtypeStruct((t, n), BF16),
        compiler_params=pltpu.CompilerParams(
            dimension_semantics=("arbitrary",), vmem_limit_bytes=VMEM_LIMIT),
        name="norm_in_proj",
    )(xa, xb, g, w)


SCAN_UNROLL = 4


def _mixer_kernel(z_ref, h0_ref, c4_ref, c3_ref, cw_ref, cb_ref, wg_ref, ba_ref, bx_ref, lam_ref,
                  sw_ref, gl_ref, gs_ref, o_ref, tail_ref,
                  xbuf, ubuf, a_s, b_s, h_s, hcar, *, tt, reset_first):
    t = pl.program_id(1)

    @pl.when(t == 0)
    def _():
        xbuf[0:SUBLANES, :] = c4_ref[0]
        ubuf[0:SUBLANES, :] = c3_ref[0]
        hcar[...] = jnp.broadcast_to(h0_ref[0], (SUBLANES, D_LRU))

    @pl.when(t > 0)
    def _():
        xbuf[0:SUBLANES, :] = xbuf[tt:tt + SUBLANES, :]
        ubuf[0:SUBLANES, :] = ubuf[tt:tt + SUBLANES, :]

    xbuf[SUBLANES:SUBLANES + tt, :] = z_ref[:, 0:D_LRU].astype(F32)
    ubuf[SUBLANES:SUBLANES + tt, :] = (z_ref[:, 2 * D_LRU + D_SC:2 * D_LRU + 2 * D_SC].astype(F32)
                                       * z_ref[:, 2 * D_LRU + 2 * D_SC:D_IN].astype(F32))

    nlam = -lam_ref[...]
    sp = jnp.maximum(nlam, 0.0) + jnp.log1p(jnp.exp(-jnp.abs(nlam)))

    row = lax.broadcasted_iota(jnp.int32, (tt, LANES), 0)
    for j in range(D_LRU // LANES):
        sl = slice(LANES * j, LANES * (j + 1))
        base = SUBLANES - (LRU_CONV - 1)
        xc = xbuf[base:base + tt, sl] * cw_ref[0:1, sl]
        for k in range(1, LRU_CONV):
            xc = xc + xbuf[base + k:base + k + tt, sl] * cw_ref[k:k + 1, sl]
        xc = xc + cb_ref[:, sl]
        gates = jnp.dot(xc.astype(BF16), wg_ref[j], preferred_element_type=F32)
        r = jax.nn.sigmoid(gates[:, 0:LANES] + ba_ref[:, sl])
        i = jax.nn.sigmoid(gates[:, LANES:2 * LANES] + bx_ref[:, sl])
        log_a = (-LRU_C * r) * sp[:, sl]
        a = jnp.exp(log_a)
        mult = jnp.sqrt(_neg_expm1(2.0 * log_a, a * a))
        if reset_first:
            mult = jnp.where(jnp.logical_and(row == 0, t == 0), 1.0, mult)
        a_s[:, sl] = a
        b_s[:, sl] = mult * (i * xc)

    srow = lax.broadcasted_iota(jnp.int32, (SUBLANES, D_LRU), 0)

    def group(gi, h_before):
        r0 = pl.multiple_of(gi * SUBLANES, SUBLANES)
        av = a_s[pl.ds(r0, SUBLANES), :]
        bv = b_s[pl.ds(r0, SUBLANES), :]
        for d in (1, 2, 4):
            keep = srow >= d
            a_sh = jnp.where(keep, pltpu.roll(av, d, 0), 1.0)
            b_sh = jnp.where(keep, pltpu.roll(bv, d, 0), 0.0)
            bv = av * b_sh + bv
            av = av * a_sh
        hprev = jnp.broadcast_to(h_before[SUBLANES - 1:SUBLANES, :], (SUBLANES, D_LRU))
        hv = bv + av * hprev
        h_s[pl.ds(r0, SUBLANES), :] = hv
        return hv

    hcar[...] = lax.fori_loop(0, tt // SUBLANES, group, hcar[...], unroll=SCAN_UNROLL)

    rc = min(tt, 32)
    for c in range(tt // rc):
        rs = slice(rc * c, rc * (c + 1))
        o_lru = h_s[rs, :] * _gelu_tanh(z_ref[rs, D_LRU:2 * D_LRU].astype(F32))
        o_ref[rs, 0:D_LRU] = _rms(o_lru, gl_ref[...]).astype(BF16)
        base = SUBLANES - (SC_CONV - 1) + rc * c
        uc = ubuf[base:base + rc, :] * sw_ref[0:1, :]
        for k in range(1, SC_CONV):
            uc = uc + ubuf[base + k:base + k + rc, :] * sw_ref[k:k + 1, :]
        o_sc = z_ref[rs, 2 * D_LRU:2 * D_LRU + D_SC].astype(F32) * uc
        o_ref[rs, D_LRU:D_LRU + D_SC] = _rms(o_sc, gs_ref[...]).astype(BF16)

    tail_ref[0, 0] = h_s[tt - SUBLANES:tt, :]
    tail_ref[0, 1] = xbuf[tt:tt + SUBLANES, :]
    tail_ref[0, 2] = ubuf[tt:tt + SUBLANES, :]


def _mixer(z, row0, nseq, tseq, tt, h0, c4, c3, params, reset_first):
    nt = tseq // tt
    blk0 = row0 // tt
    cw, cb, wg, ba, bx, lam, sw, gl, gs = params
    full = lambda a: pl.BlockSpec(a.shape, lambda s, t: (0,) * a.ndim)
    return pl.pallas_call(
        functools.partial(_mixer_kernel, tt=tt, reset_first=reset_first),
        grid=(nseq, nt),
        in_specs=[
            pl.BlockSpec((tt, D_IN), lambda s, t: (blk0 + s * nt + t, 0)),
            pl.BlockSpec((1, 1, D_LRU), lambda s, t: (s, 0, 0)),
            pl.BlockSpec((1, SUBLANES, D_LRU), lambda s, t: (s, 0, 0)),
            pl.BlockSpec((1, SUBLANES, D_SC), lambda s, t: (s, 0, 0)),
            full(cw), full(cb), full(wg), full(ba), full(bx), full(lam), full(sw), full(gl), full(gs),
        ],
        out_specs=[
            pl.BlockSpec((tt, D_LRU + D_SC), lambda s, t: (s * nt + t, 0)),
            pl.BlockSpec((1, 3, SUBLANES, D_LRU), lambda s, t: (s, 0, 0, 0)),
        ],
        out_shape=[
            jax.ShapeDtypeStruct((nseq * tseq, D_LRU + D_SC), BF16),
            jax.ShapeDtypeStruct((nseq, 3, SUBLANES, D_LRU), F32),
        ],
        scratch_shapes=[
            pltpu.VMEM((tt + 2 * SUBLANES, D_LRU), F32),
            pltpu.VMEM((tt + 2 * SUBLANES, D_SC), F32),
            pltpu.VMEM((tt, D_LRU), F32),
            pltpu.VMEM((tt, D_LRU), F32),
            pltpu.VMEM((tt, D_LRU), F32),
            pltpu.VMEM((SUBLANES, D_LRU), F32),
        ],
        compiler_params=pltpu.CompilerParams(
            dimension_semantics=("arbitrary", "arbitrary"), vmem_limit_bytes=VMEM_LIMIT),
        name="mixer_reset" if reset_first else "mixer_cont",
    )(z, h0, c4, c3, cw, cb, wg, ba, bx, lam, sw, gl, gs)


def _out_q_kernel(oa_ref, ob_ref, xa_ref, xb_ref, wout_ref, g_ref, wq_ref, x1_ref, xnt_ref, q_ref,
                  *, n_a):
    in_a = pl.program_id(0) < n_a

    @pl.when(in_a)
    def _():
        x1_ref[...] = xa_ref[...] + jnp.dot(oa_ref[...], wout_ref[...], preferred_element_type=F32)

    @pl.when(jnp.logical_not(in_a))
    def _():
        x1_ref[...] = xb_ref[...] + jnp.dot(ob_ref[...], wout_ref[...], preferred_element_type=F32)

    x1 = x1_ref[...]
    xn32 = _rms(x1, g_ref[...])
    xn = xn32.astype(BF16)
    xnt_ref[...] = xn32.T.astype(BF16)
    q = jnp.dot(xn, wq_ref[...], preferred_element_type=F32).astype(BF16)
    for c in range(2 * PEER_HEADS):
        q_ref[c] = q[:, D_HALF * c:D_HALF * (c + 1)]


def _out_q(oa, ob, xa, xb, w_out, g, w_q, tm):
    n_a, n_b = xa.shape[0] // tm, xb.shape[0] // tm
    t = xa.shape[0] + xb.shape[0]
    const = lambda a: pl.BlockSpec(a.shape, lambda i: (0,) * a.ndim, pipeline_mode=pl.Buffered(1))
    return pl.pallas_call(
        functools.partial(_out_q_kernel, n_a=n_a),
        grid=(t // tm,),
        in_specs=_two_group_specs(n_a, n_b, tm, D_MODEL) + _two_group_specs(n_a, n_b, tm, D_MODEL) + [
            const(w_out), const(g), const(w_q),
        ],
        out_specs=[
            pl.BlockSpec((tm, D_MODEL), lambda i: (i, 0)),
            pl.BlockSpec((D_MODEL, tm), lambda i: (0, i)),
            pl.BlockSpec((2 * PEER_HEADS, tm, D_HALF), lambda i: (0, i, 0)),
        ],
        out_shape=[
            jax.ShapeDtypeStruct((t, D_MODEL), F32),
            jax.ShapeDtypeStruct((D_MODEL, t), BF16),
            jax.ShapeDtypeStruct((2 * PEER_HEADS, t, D_HALF), BF16),
        ],
        compiler_params=pltpu.CompilerParams(
            dimension_semantics=("arbitrary",), vmem_limit_bytes=VMEM_LIMIT),
        name="out_proj_q_proj",
    )(oa, ob, xa, xb, w_out, g, w_q)


N_RANK = PEER_TOPK + 1
BF16_ROWS = 2 * SUBLANES
RANK_SCALE = 2.0 ** 20
KEY_GROUPS = N_KEYS // BF16_ROWS


def _top_values(s, n, count_scale=None):
    vals = [jnp.max(s, axis=0, keepdims=True)]
    count = jnp.zeros_like(s)
    for r in range(1, n):
        below = s < vals[-1]
        if count_scale is not None:
            count = jnp.where(below, r * count_scale, count)
        vals.append(jnp.max(jnp.where(below, s, NEG_INF), axis=0, keepdims=True))
    if count_scale is None:
        return vals
    return vals, jnp.where(s < vals[-1], n * count_scale, count)


def _peer_select_kernel(q1_ref, q2_ref, k1_ref, k2_ref, cnt_ref, e1_ref, rank_ref, e2_ref, *, tb):
    s1_all = _nt_dot(k1_ref[...], q1_ref[0])
    s2_all = _nt_dot(k2_ref[...], q2_ref[0])
    for l in range(tb // LANES):
        sl = slice(LANES * l, LANES * (l + 1))
        s1 = s1_all[:, sl]
        s2 = s2_all[:, sl]
        v1 = _top_values(s1, N_RANK)
        v2, rank = _top_values(s2, N_RANK, RANK_SCALE)
        pad = [jnp.full((1, LANES), NEG_INF, F32)] * (3 * SUBLANES - N_RANK)
        v1_col = jnp.concatenate(v1 + pad, axis=0)
        v2_col = jnp.concatenate(v2 + pad, axis=0)
        cands = ([v1[0] + v2_col] + [v1[a] + v2_col[0:SUBLANES] for a in range(1, SUBLANES)]
                 + [v1_col[SUBLANES:] + v2[0]])
        cand = jnp.concatenate(cands, axis=0)
        c = _top_values(cand, N_RANK)
        theta = 0.5 * (c[PEER_TOPK - 1] + c[PEER_TOPK])
        z = jnp.sum(jnp.where(cand >= theta, jnp.exp(cand - c[0]), 0.0), axis=0, keepdims=True)
        thr = theta - s1
        cnt = jnp.zeros_like(s1)
        for b in range(PEER_TOPK):
            cnt = jnp.where(v2[b] >= thr, (b + 1) * RANK_SCALE, cnt)
        cnt_ref[0, :, sl] = cnt
        e1_ref[0, :, sl] = jnp.exp(s1 - v1[0]) / z
        rank_ref[0, l] = rank.astype(BF16).reshape(KEY_GROUPS, BF16_ROWS, LANES)
        e2_ref[0, l] = jnp.exp(s2 - v2[0]).astype(BF16).reshape(KEY_GROUPS, BF16_ROWS, LANES)


def _peer_select(q, k1, k2, tb):
    t = q.shape[1]
    rows = jax.ShapeDtypeStruct((PEER_HEADS, N_KEYS, t), F32)
    rows_spec = pl.BlockSpec((1, N_KEYS, tb), lambda i, h: (h, 0, i))
    tiles = jax.ShapeDtypeStruct((PEER_HEADS, t // LANES, KEY_GROUPS, BF16_ROWS, LANES), BF16)
    tiles_spec = pl.BlockSpec((1, tb // LANES, KEY_GROUPS, BF16_ROWS, LANES), lambda i, h: (h, i, 0, 0, 0))
    return pl.pallas_call(
        functools.partial(_peer_select_kernel, tb=tb),
        grid=(t // tb, PEER_HEADS),
        in_specs=[
            pl.BlockSpec((1, tb, D_HALF), lambda i, h: (2 * h, i, 0)),
            pl.BlockSpec((1, tb, D_HALF), lambda i, h: (2 * h + 1, i, 0)),
            pl.BlockSpec((N_KEYS, D_HALF), lambda i, h: (0, 0)),
            pl.BlockSpec((N_KEYS, D_HALF), lambda i, h: (0, 0)),
        ],
        out_specs=[rows_spec, rows_spec, tiles_spec, tiles_spec],
        out_shape=[rows, rows, tiles, tiles],
        compiler_params=pltpu.CompilerParams(
            dimension_semantics=("arbitrary", "arbitrary"), vmem_limit_bytes=VMEM_LIMIT),
        name="peer_select",
    )(q, q, k1, k2)


def _peer_dense_kernel(u_ref, vt_ref, xnt_ref, x1_ref, cnt_ref, e1_ref, rank_ref, e2_ref, g_ref,
                       ya_ref, yb_ref, act0_ref, act1_ref, wt0_ref, wt1_ref, acc_ref, *, tb, eb, ne, n_a,
                       strip_rows, col_width):
    s = pl.program_id(0)
    nl = tb // LANES

    @pl.when(s == 0)
    def _():
        act1_ref[...] = jnp.zeros_like(act1_ref)
        wt0_ref[...] = jnp.zeros_like(wt0_ref)
        acc_ref[...] = jnp.zeros_like(acc_ref)

    assert eb // N_KEYS == SUBLANES
    gl = col_width // LANES
    assert nl % gl == 0
    pieces_per_group = SUBLANES * gl
    c_every = pieces_per_group * strip_rows // D_MODEL
    a_every = pieces_per_group * strip_rows // eb
    assert c_every >= 2 and a_every > c_every // 2
    pack =lambda row: jnp.broadcast_to(row, (BF16_ROWS, LANES)).astype(BF16)[None]

    def stages(p):
        act_w, act_r = (act0_ref, act1_ref) if p == 0 else (act1_ref, act0_ref)
        wt_r, wt_w = (wt0_ref, wt1_ref) if p == 0 else (wt1_ref, wt0_ref)
        piece = 0
        for grp in range(nl // gl):
            lanes = range(gl * grp, gl * (grp + 1))
            cols = slice(LANES * gl * grp, LANES * gl * (grp + 1))
            wt_cols = jnp.concatenate([wt_r[l] for l in lanes], axis=1)
            for l in lanes:
                sl = slice(LANES * l, LANES * (l + 1))
                for ib in range(SUBLANES):
                    rows = slice(N_KEYS * ib, N_KEYS * (ib + 1))
                    k = piece % pieces_per_group
                    if k % c_every == 0:
                        j = k // c_every
                        orows = slice(strip_rows * j, strip_rows * (j + 1))
                        acc_ref[orows, cols] += jnp.dot(vt_ref[0, orows, :], wt_cols,
                                                        preferred_element_type=F32)
                    if k % a_every == c_every // 2:
                        a = k // a_every
                        arows = slice(strip_rows * a, strip_rows * (a + 1))
                        strip = jnp.dot(u_ref[arows, :], xnt_ref[:, cols], preferred_element_type=F32)
                        for n, ll in enumerate(lanes):
                            act_w[ll, arows, :] = strip[:, LANES * n:LANES * (n + 1)]
                    w = jnp.zeros((KEY_GROUPS, BF16_ROWS, LANES), BF16)
                    for h in range(PEER_HEADS):
                        cnt = pack(cnt_ref[h, ib:ib + 1, sl])
                        e1 = pack(e1_ref[h, ib:ib + 1, sl])
                        w = w + jnp.maximum(jnp.minimum(e1 * e2_ref[h, l], cnt - rank_ref[h, l]), 0.0)
                    g = _gelu_tanh(act_r[l, rows, :]).astype(BF16).reshape(KEY_GROUPS, BF16_ROWS, LANES)
                    wt_w[l, rows, :] = (w * g).reshape(N_KEYS, LANES)
                    piece += 1

    for parity in range(2):
        pl.when(s % 2 == parity)(functools.partial(stages, parity))

    @pl.when(jnp.logical_and((s - 1) % ne == 0, s > 1))
    def _():
        x2 = x1_ref[...] + acc_ref[...].T
        acc_ref[...] = jnp.zeros_like(acc_ref)
        y = _rms(x2, g_ref[...])
        done_tile = (s - 2) // ne

        @pl.when(done_tile < n_a)
        def _():
            ya_ref[...] = y

        @pl.when(done_tile >= n_a)
        def _():
            yb_ref[...] = y


def _peer_dense(u, vt, xnt, x1, cnt, e1, rank, e2, g, tb, eb, n_a, strip_rows, col_width):
    t = x1.shape[0]
    nt = t // tb
    n_b = nt - n_a
    ne = N_EXPERTS // eb
    tile = lambda s, lag: jnp.clip((s - lag) // ne, 0, nt - 1)
    rows_spec = pl.BlockSpec((PEER_HEADS, SUBLANES, tb), lambda s: (0, (s + ne - 1) % ne, tile(s, 1)))
    tiles_spec = pl.BlockSpec((PEER_HEADS, tb // LANES, KEY_GROUPS, BF16_ROWS, LANES),
                              lambda s: (0, tile(s, 1), 0, 0, 0))
    act_buf = pltpu.VMEM((tb // LANES, eb, LANES), F32)
    wt_buf = pltpu.VMEM((tb // LANES, eb, LANES), BF16)
    return pl.pallas_call(
        functools.partial(_peer_dense_kernel, tb=tb, eb=eb, ne=ne, n_a=n_a, strip_rows=strip_rows,
                          col_width=col_width),
        grid=(nt * ne + 2,),
        in_specs=[
            pl.BlockSpec((eb, D_MODEL), lambda s: (s % ne, 0)),
            pl.BlockSpec((1, D_MODEL, eb), lambda s: ((s + ne - 2) % ne, 0, 0)),
            pl.BlockSpec((D_MODEL, tb), lambda s: (0, tile(s, 0))),
            pl.BlockSpec((tb, D_MODEL), lambda s: (tile(s, 2), 0), pipeline_mode=pl.Buffered(1)),
            rows_spec, rows_spec, tiles_spec, tiles_spec,
            pl.BlockSpec((1, D_MODEL), lambda s: (0, 0)),
        ],
        out_specs=[
            pl.BlockSpec((tb, D_MODEL), lambda s: (jnp.minimum(tile(s, 2), n_a - 1), 0)),
            pl.BlockSpec((tb, D_MODEL), lambda s: (jnp.clip(tile(s, 2) - n_a, 0, n_b - 1), 0)),
        ],
        out_shape=[jax.ShapeDtypeStruct((n_a * tb, D_MODEL), F32),
                   jax.ShapeDtypeStruct((n_b * tb, D_MODEL), F32)],
        scratch_shapes=[act_buf, act_buf, wt_buf, wt_buf, pltpu.VMEM((D_MODEL, tb), F32)],
        compiler_params=pltpu.CompilerParams(
            dimension_semantics=("arbitrary",), vmem_limit_bytes=VMEM_LIMIT),
        name="peer_dense",
    )(u, vt, xnt, x1, cnt, e1, rank, e2, g)


def _gate_weights(w_a, w_x):
    def blockdiag(w):
        w = w.reshape(LRU_HEADS // 2, 2, LRU_HEAD_DIM, LRU_HEAD_DIM)
        zero = jnp.zeros_like(w[:, 0])
        top = jnp.concatenate([w[:, 0], zero], axis=-1)
        bot = jnp.concatenate([zero, w[:, 1]], axis=-1)
        return jnp.concatenate([top, bot], axis=-2)
    return jnp.concatenate([blockdiag(w_a), blockdiag(w_x)], axis=-1).astype(BF16)


def _right_align(cache, rows):
    n, w, c = cache.shape
    return jnp.concatenate([jnp.zeros((n, rows - w, c), cache.dtype), cache], axis=1)


def kernel(x_prompt, x_sample, state_lru, cache_conv_lru, cache_conv_short, g_mix, w_in, conv_lru_w,
           conv_lru_b, lru_w_a, lru_b_a, lru_w_x, lru_b_x, lru_lambda, conv_sc_w, g_out_lru, g_out_sc,
           w_out, g_ffn, peer_w_q, peer_k1, peer_k2, peer_u, peer_v, g_final):
    depth = g_mix.shape[0]
    assert depth == 1, "the final rmsnorm is fused into the (single) layer's PEER kernel"
    nb, seq, _ = x_prompt.shape
    ndec, dseq, _ = x_sample.shape
    n_prompt = nb * seq
    xa = x_prompt.reshape(n_prompt, D_MODEL)
    xb = x_sample.reshape(ndec * dseq, D_MODEL)

    outs = {k: [] for k in ("hp", "c4p", "c3p", "hs", "c4s", "c3s")}
    for l in range(depth):
        row = lambda a: a[l].reshape(1, -1)
        z = _norm_matmul(xa, xb, row(g_mix), w_in[l].astype(BF16), tm=512)

        params = (conv_lru_w[l], row(conv_lru_b), _gate_weights(lru_w_a[l], lru_w_x[l]), row(lru_b_a),
                  row(lru_b_x), row(lru_lambda), conv_sc_w[l], row(g_out_lru), row(g_out_sc))
        o_p, tail_p = _mixer(
            z, 0, nb, seq, 256, jnp.zeros((nb, 1, D_LRU), F32), jnp.zeros((nb, SUBLANES, D_LRU), F32),
            jnp.zeros((nb, SUBLANES, D_SC), F32), params, True)
        o_s, tail_s = _mixer(
            z, n_prompt, ndec, dseq, dseq, state_lru[l].reshape(ndec, 1, D_LRU),
            _right_align(cache_conv_lru[l], SUBLANES), _right_align(cache_conv_short[l], SUBLANES),
            params, False)
        x1, xnt, q = _out_q(o_p, o_s, xa, xb, w_out[l].astype(BF16), row(g_ffn), peer_w_q[l].astype(BF16),
                            tm=512)
        cnt, e1, rank, e2 = _peer_select(q, peer_k1[l].astype(BF16), peer_k2[l].astype(BF16), tb=512)
        eb = SUBLANES * N_KEYS
        vt = peer_v[l].reshape(N_EXPERTS // eb, eb, D_MODEL).transpose(0, 2, 1).astype(BF16)
        xa, xb = _peer_dense(peer_u[l].astype(BF16), vt, xnt, x1, cnt, e1, rank, e2,
                             g_final.reshape(1, -1), tb=512, eb=eb, n_a=n_prompt // 512,
                             strip_rows=256, col_width=256)

        for name, tail, i, w in (("hp", tail_p, 0, 1), ("c4p", tail_p, 1, LRU_CONV - 1),
                                 ("c3p", tail_p, 2, SC_CONV - 1), ("hs", tail_s, 0, 1),
                                 ("c4s", tail_s, 1, LRU_CONV - 1), ("c3s", tail_s, 2, SC_CONV - 1)):
            outs[name].append(tail[:, i, SUBLANES - w:, :])

    y_prompt = xa.reshape(x_prompt.shape)
    y_sample = xb.reshape(x_sample.shape)
    hp = jnp.stack(outs["hp"])[:, :, 0, :]
    hs = jnp.stack(outs["hs"])[:, :, 0, :]
    return (y_prompt, y_sample, hp, jnp.stack(outs["c4p"]), jnp.stack(outs["c3p"]),
            hs, jnp.stack(outs["c4s"]), jnp.stack(outs["c3s"]))
```
